```python
import jax, jax.numpy as jnp
from jax import lax
import numpy as np

D_MODEL = 1024
BATCH = 8
SEQ = 2048
DEPTH = 4

N_MIXERS = 4
EXPAND = 2
D_INNER = EXPAND * D_MODEL
FNET_GROUPS = 4
FNET_GROUP_DIM = D_INNER // FNET_GROUPS
CONF_KERNEL = 31
POOL_WINDOWS = (2, 4, 8, 16)
POOL_GROUPS = len(POOL_WINDOWS)
POOL_GROUP_DIM = D_INNER // POOL_GROUPS
SHORT_CONV_WIDTH = 3
RMS_EPS = 1e-6
LN_EPS = 1e-5

kernel_name = "hybrid_fourier_conformer_pool_shortconv_encoder"


def _n_layers_of(m):
    return len(range(m, DEPTH, N_MIXERS))


def rms_norm(x, g):
    xf = x.astype(jnp.float32)
    y = xf * lax.rsqrt(jnp.mean(xf * xf, axis=-1, keepdims=True) + RMS_EPS)
    return (y * g.astype(jnp.float32)).astype(x.dtype)


def depthwise_conv(h, w):
    k = w.shape[0]
    return lax.conv_general_dilated(
        h, w[:, None, :].astype(h.dtype), window_strides=(1,),
        padding=[(k // 2, k // 2)], dimension_numbers=('NWC', 'WIO', 'NWC'),
        feature_group_count=h.shape[-1])


def fourier_mix(u, w_mix, b_mix):
    b_, s_, _ = u.shape
    ug = u.reshape(b_, s_, FNET_GROUPS, FNET_GROUP_DIM).astype(jnp.float32)
    f = jnp.fft.fftn(ug, axes=(1, 3), norm="ortho").real.astype(u.dtype)
    y = jnp.einsum('bsgc,gcd->bsgd', f, w_mix) + b_mix
    return y.reshape(b_, s_, D_INNER)


def conformer_conv(u, dw_w, dw_b, ln_g, ln_b):
    a, gate = jnp.split(u, 2, axis=-1)
    h = a * jax.nn.sigmoid(gate)
    h = depthwise_conv(h, dw_w) + dw_b
    hf = h.astype(jnp.float32)
    mu = jnp.mean(hf, axis=-1, keepdims=True)
    var = jnp.mean(jnp.square(hf - mu), axis=-1, keepdims=True)
    hn = ((hf - mu) * lax.rsqrt(var + LN_EPS) * ln_g.astype(jnp.float32)
          + ln_b.astype(jnp.float32)).astype(u.dtype)
    return jax.nn.silu(hn)


def multiscale_pool(u, w_grp, scale):
    b_, s_, _ = u.shape
    uf = u.astype(jnp.float32)
    cs = jnp.concatenate([jnp.zeros((b_, 1, D_INNER), jnp.float32),
                          jnp.cumsum(uf, axis=1)], axis=1)
    t = jnp.arange(s_)
    outs = []
    for g, w in enumerate(POOL_WINDOWS):
        left = w // 2
        right = w - 1 - left
        lo = jnp.clip(t - left, 0, s_)
        hi = jnp.clip(t + right + 1, 0, s_)
        sl = slice(g * POOL_GROUP_DIM, (g + 1) * POOL_GROUP_DIM)
        csg = cs[..., sl]
        cnt = (hi - lo).astype(jnp.float32)[None, :, None]
        mean = (jnp.take(csg, hi, axis=1) - jnp.take(csg, lo, axis=1)) / cnt
        outs.append(mean - uf[..., sl])
    p = jnp.stack(outs, axis=2).astype(u.dtype)
    y = jnp.einsum('bsgc,gcd->bsgd', p, w_grp).reshape(b_, s_, D_INNER)
    return y * scale


def short_gated_conv(u, conv_w):
    bg, cg, h = jnp.split(u, 3, axis=-1)
    return bg * depthwise_conv(cg * h, conv_w)


def setup_inputs(seed: int = 0) -> dict:
    key = jax.random.key(seed)
    ks = jax.random.split(key, 20)
    f32 = jnp.float32
    nA, nB, nC, nD = (_n_layers_of(m) for m in range(N_MIXERS))
    E, D = D_INNER, D_MODEL
    nrm = lambda k, shape, s: (jax.random.normal(k, shape, f32) * s)
    return {
        "x": jax.random.normal(ks[0], (BATCH, SEQ, D), f32),
        "norm_g": 1.0 + nrm(ks[1], (DEPTH, D), 0.05),
        "w_out": nrm(ks[2], (DEPTH, E, D), E ** -0.5),
        "final_g": 1.0 + nrm(ks[3], (D,), 0.05),
        "fnet_w_in": nrm(ks[4], (nA, D, 2 * E), D ** -0.5),
        "fnet_w_mix": nrm(ks[5], (nA, FNET_GROUPS, FNET_GROUP_DIM, FNET_GROUP_DIM), FNET_GROUP_DIM ** -0.5),
        "fnet_b_mix": nrm(ks[6], (nA, FNET_GROUPS, FNET_GROUP_DIM), 0.02),
        "conf_w_in": nrm(ks[7], (nB, D, 3 * E), D ** -0.5),
        "conf_dw_w": nrm(ks[8], (nB, CONF_KERNEL, E), CONF_KERNEL ** -0.5),
        "conf_dw_b": nrm(ks[9], (nB, E), 0.02),
        "conf_ln_g": 1.0 + nrm(ks[10], (nB, E), 0.05),
        "conf_ln_b": nrm(ks[11], (nB, E), 0.02),
        "pool_w_in": nrm(ks[12], (nC, D, 2 * E), D ** -0.5),
        "pool_w_grp": nrm(ks[13], (nC, POOL_GROUPS, POOL_GROUP_DIM, POOL_GROUP_DIM), POOL_GROUP_DIM ** -0.5),
        "pool_scale": 1.0 + nrm(ks[14], (nC, E), 0.1),
        "sc_w_in": nrm(ks[15], (nD, D, 4 * E), D ** -0.5),
        "sc_conv_w": nrm(ks[16], (nD, SHORT_CONV_WIDTH, E), SHORT_CONV_WIDTH ** -0.5),
    }


def reference(x, norm_g, w_out, final_g, fnet_w_in, fnet_w_mix, fnet_b_mix,
              conf_w_in, conf_dw_w, conf_dw_b, conf_ln_g, conf_ln_b,
              pool_w_in, pool_w_grp, pool_scale, sc_w_in, sc_conv_w):
    for i in range(DEPTH):
        m, j = i % N_MIXERS, i // N_MIXERS
        xn = rms_norm(x, norm_g[i])
        if m == 0:
            hp = jnp.einsum('bsd,de->bse', xn, fnet_w_in[j])
            u = fourier_mix(hp[..., :D_INNER], fnet_w_mix[j], fnet_b_mix[j])
        elif m == 1:
            hp = jnp.einsum('bsd,de->bse', xn, conf_w_in[j])
            u = conformer_conv(hp[..., :2 * D_INNER], conf_dw_w[j], conf_dw_b[j],
                               conf_ln_g[j], conf_ln_b[j])
        elif m == 2:
            hp = jnp.einsum('bsd,de->bse', xn, pool_w_in[j])
            u = multiscale_pool(hp[..., :D_INNER], pool_w_grp[j], pool_scale[j])
        else:
            hp = jnp.einsum('bsd,de->bse', xn, sc_w_in[j])
            u = short_gated_conv(hp[..., :3 * D_INNER], sc_conv_w[j])
        z = hp[..., -D_INNER:]
        x = x + jnp.einsum('bse,ed->bsd', u * jax.nn.silu(z), w_out[i])
    return rms_norm(x, final_g)
```

```python
import jax
import jax.numpy as jnp
from jax import lax
from jax.experimental import pallas as pl
from jax.experimental.pallas import tpu as pltpu

D_MODEL = 1024
D_INNER = 2048
BATCH = 8
SEQ = 2048
FNET_GROUPS = 4
FNET_GROUP_DIM = D_INNER // FNET_GROUPS
CONF_KERNEL = 31
POOL_WINDOWS = (2, 4, 8, 16)
SHORT_CONV_WIDTH = 3
RMS_EPS = 1e-6
LN_EPS = 1e-5

TS = 512
HALO = 16
EXT = TS + 2 * HALO
CW = 512
NC = D_INNER // CW
LANES = 128
SUBLANES = 8
LT = CW // LANES
NLT = D_INNER // LANES
VMEM_LIMIT = 56 * 1024 * 1024

F32 = jnp.float32
BF16 = jnp.bfloat16


def _dot(a, b):
    return jnp.dot(a, b, preferred_element_type=F32)


def _rms(v, g):
    ms = jnp.mean(v * v, axis=-1, keepdims=True)
    return v * lax.rsqrt(ms + RMS_EPS) * g


def _silu(v):
    return v * jax.nn.sigmoid(v)


def _fill_xn_ext(xm_ref, xp_ref, xq_ref, g_ref, xn_s):
    i = pl.program_id(1)
    g = g_ref[...]
    prev = jnp.where(i > 0, _rms(xp_ref[0], g), 0.0)
    nxt = jnp.where(i < pl.num_programs(1) - 1, _rms(xq_ref[0], g), 0.0)
    xn_s[0:HALO, :] = prev.astype(BF16)
    xn_s[HALO:HALO + TS, :] = _rms(xm_ref[0], g).astype(BF16)
    xn_s[HALO + TS:EXT, :] = nxt.astype(BF16)


def _store_lane_tiles(slab_ref, first, val):
    for lt in range(LT):
        slab_ref[first + lt] = val[:, lt * LANES:(lt + 1) * LANES]


def _conf_kernel(xm_ref, xp_ref, xq_ref, g_ref, w_in_ref, dw_w_ref, dw_b_ref, ln_g_ref, ln_b_ref,
                 w_out_ref, o_ref, xn_s, h_s, hc_s):
    _fill_xn_ext(xm_ref, xp_ref, xq_ref, g_ref, xn_s)

    def glu_chunk(c, carry):
        xe = xn_s[...]
        a = _dot(xe, w_in_ref[c])
        gate = _dot(xe, w_in_ref[NC + c])
        _store_lane_tiles(h_s, c * LT, a * jax.nn.sigmoid(gate))
        return carry
    lax.fori_loop(0, NC, glu_chunk, 0)

    def conv_lane_tile(l, carry):
        wk = [dw_w_ref[l, k:k + 1, :] for k in range(CONF_KERNEL)]
        bias = dw_b_ref[l]
        for j in range(TS // SUBLANES):
            r0 = j * SUBLANES
            base = r0 + HALO - CONF_KERNEL // 2
            acc = h_s[l, base:base + SUBLANES, :] * wk[0]
            for k in range(1, CONF_KERNEL):
                acc = acc + h_s[l, base + k:base + k + SUBLANES, :] * wk[k]
            hc_s[l, r0:r0 + SUBLANES, :] = acc + bias
        return carry
    lax.fori_loop(0, NLT, conv_lane_tile, 0)

    tot = hc_s[0]
    for l in range(1, NLT):
        tot = tot + hc_s[l]
    mu = jnp.sum(tot, axis=-1, keepdims=True) * (1.0 / D_INNER)
    sq = jnp.square(hc_s[0] - mu)
    for l in range(1, NLT):
        sq = sq + jnp.square(hc_s[l] - mu)
    rstd = lax.rsqrt(jnp.sum(sq, axis=-1, keepdims=True) * (1.0 / D_INNER) + LN_EPS)

    o_ref[0] = xm_ref[0]

    def out_chunk(c, carry):
        hc = jnp.concatenate([hc_s[c * LT + lt] for lt in range(LT)], axis=-1)
        hn = (hc - mu) * rstd * ln_g_ref[c] + ln_b_ref[c]
        z = _dot(xn_s[HALO:HALO + TS, :], w_in_ref[2 * NC + c])
        v = (_silu(hn) * _silu(z)).astype(BF16)
        o_ref[0] += _dot(v, w_out_ref[c])
        return carry
    lax.fori_loop(0, NC, out_chunk, 0)


def _pool_kernel(xm_ref, xp_ref, xq_ref, g_ref, w_in_ref, w_grp_ref, scale_ref, w_out_ref, o_ref,
                 xn_s, u_s, p_s):
    _fill_xn_ext(xm_ref, xp_ref, xq_ref, g_ref, xn_s)
    t = pl.program_id(1) * TS + lax.broadcasted_iota(jnp.int32, (TS, 1), 0)
    o_ref[0] = xm_ref[0]

    def chunk(c, carry):
        _store_lane_tiles(u_s, 0, _dot(xn_s[...], w_in_ref[c]))
        for grp, w in enumerate(POOL_WINDOWS):
            left = w // 2
            right = w - 1 - left

            @pl.when(c == grp)
            def _():
                cnt = jnp.minimum(t + right + 1, SEQ) - jnp.maximum(t - left, 0)
                inv_cnt = 1.0 / cnt.astype(F32)
                for lt in range(LT):
                    wsum = u_s[lt, HALO - left:HALO - left + TS, :]
                    for k in range(1, w):
                        wsum = wsum + u_s[lt, HALO - left + k:HALO - left + k + TS, :]
                    p_s[:, lt * LANES:(lt + 1) * LANES] = (
                        wsum * inv_cnt - u_s[lt, HALO:HALO + TS, :]).astype(BF16)
        y = _dot(p_s[...], w_grp_ref[c]) * scale_ref[c]
        z = _dot(xn_s[HALO:HALO + TS, :], w_in_ref[NC + c])
        v = (y * _silu(z)).astype(BF16)
        o_ref[0] += _dot(v, w_out_ref[c])
        return carry
    lax.fori_loop(0, NC, chunk, 0)


def _sconv_kernel(xm_ref, xp_ref, xq_ref, g_ref, w_in_ref, conv_w_ref, w_out_ref, fg_ref, o_ref,
                  xn_s, ch_s):
    _fill_xn_ext(xm_ref, xp_ref, xq_ref, g_ref, xn_s)
    o_ref[0] = xm_ref[0]

    def chunk(c, carry):
        xe = xn_s[...]
        cg = _dot(xe, w_in_ref[NC + c])
        hh = _dot(xe, w_in_ref[2 * NC + c])
        _store_lane_tiles(ch_s, 0, cg * hh)
        cw = conv_w_ref[c]
        base = HALO - SHORT_CONV_WIDTH // 2
        parts = []
        for lt in range(LT):
            ls = slice(lt * LANES, (lt + 1) * LANES)
            conv = ch_s[lt, base:base + TS, :] * cw[0:1, ls]
            for k in range(1, SHORT_CONV_WIDTH):
                conv = conv + ch_s[lt, base + k:base + k + TS, :] * cw[k:k + 1, ls]
            parts.append(conv)
        conv = jnp.concatenate(parts, axis=-1)
        xm = xn_s[HALO:HALO + TS, :]
        bg = _dot(xm, w_in_ref[c])
        z = _dot(xm, w_in_ref[3 * NC + c])
        v = (bg * conv * _silu(z)).astype(BF16)
        o_ref[0] += _dot(v, w_out_ref[c])
        return carry
    lax.fori_loop(0, NC, chunk, 0)
    o_ref[0] = _rms(o_ref[0], fg_ref[...])


def _fnet_mix_weights_kernel(cc_ref, sc_ref, w_ref, o_ref):
    w = w_ref[0]
    gd = FNET_GROUP_DIM
    o_ref[0, :, 0:gd] = jnp.dot(cc_ref[...], w, preferred_element_type=F32,
                                precision=lax.Precision.HIGHEST).astype(BF16)
    o_ref[0, :, gd:2 * gd] = jnp.dot(sc_ref[...], w, preferred_element_type=F32,
                                     precision=lax.Precision.HIGHEST).astype(BF16)


def _fnet_in_kernel(x_ref, g_ref, w_in_ref, ab_ref, p_ref, q_ref, gz_ref, xn_s):
    xn_s[...] = _rms(x_ref[0], g_ref[...]).astype(BF16)
    gd = FNET_GROUP_DIM

    def chunk(c, carry):
        xn = xn_s[...]
        u = _dot(xn, w_in_ref[c]).astype(BF16)
        pq = _dot(u, ab_ref[c])
        p_ref[0, c] = pq[:, 0:gd].astype(BF16)
        q_ref[0, c] = pq[:, gd:2 * gd].astype(BF16)
        gz_ref[0, c] = _silu(_dot(xn, w_in_ref[NC + c])).astype(BF16)
        return carry
    lax.fori_loop(0, NC, chunk, 0)


def _fnet_seq_dft_kernel(cs_ref, ss_ref, p_ref, q_ref, b_ref, gz_ref, v_ref):
    def row_block(r, carry):
        rows = pl.ds(pl.multiple_of(r * TS, TS), TS)
        f = _dot(cs_ref[rows, :], p_ref[0, 0]) - _dot(ss_ref[rows, :], q_ref[0, 0]) + b_ref[0]
        v_ref[0, 0, rows, :] = (f * gz_ref[0, 0, rows, :].astype(F32)).astype(BF16)
        return carry
    lax.fori_loop(0, SEQ // TS, row_block, 0)


def _out_proj_kernel(x_ref, v_ref, w_out_ref, o_ref):
    o_ref[0] = x_ref[0]

    def chunk(c, carry):
        o_ref[0] += _dot(v_ref[0, c], w_out_ref[c])
        return carry
    lax.fori_loop(0, NC, chunk, 0)


def _resident(shape):
    nd = len(shape)
    return pl.BlockSpec(shape, lambda *_: (0,) * nd, pipeline_mode=pl.Buffered(1))


def _params(sem):
    return pltpu.CompilerParams(dimension_semantics=sem, vmem_limit_bytes=VMEM_LIMIT)


def _row_tile(width):
    return pl.BlockSpec((1, TS, width), lambda b, i: (b, i, 0))


def _tiled_layer(body, name, x, consts, scratch):
    per_tile = TS // HALO
    prev = pl.BlockSpec((1, HALO, D_MODEL), lambda b, i: (b, jnp.maximum(i * per_tile - 1, 0), 0))
    nxt = pl.BlockSpec((1, HALO, D_MODEL),
                       lambda b, i: (b, jnp.minimum((i + 1) * per_tile, SEQ // HALO - 1), 0))
    return pl.pallas_call(
        body,
        grid=(BATCH, SEQ // TS),
        in_specs=[_row_tile(D_MODEL), prev, nxt] + [_resident(c.shape) for c in consts],
        out_specs=_row_tile(D_MODEL),
        out_shape=jax.ShapeDtypeStruct((BATCH, SEQ, D_MODEL), F32),
        scratch_shapes=[pltpu.VMEM((EXT, D_MODEL), BF16)] + scratch,
        compiler_params=_params(("arbitrary", "arbitrary")),
        name=name,
    )(x, x, x, *consts)


def _chunk_major(w):
    return w.reshape(D_MODEL, -1, CW).transpose(1, 0, 2).astype(BF16)


def _per_chunk(v):
    return v.reshape(NC, 1, CW)


def _dft_tables(n):
    k = jnp.arange(n, dtype=jnp.int32)
    ang = ((k[:, None] * k[None, :]) % n).astype(F32) * (2.0 * jnp.pi / n)
    return jnp.cos(ang), jnp.sin(ang)


def _fnet_layer(x, g, w_in, w_mix, b_mix, w_out):
    gd = FNET_GROUP_DIM
    ortho = 1.0 / (SEQ * gd) ** 0.5
    cc, sc = _dft_tables(gd)
    ab = pl.pallas_call(
        _fnet_mix_weights_kernel,
        grid=(FNET_GROUPS,),
        in_specs=[_resident((gd, gd)), _resident((gd, gd)),
                  pl.BlockSpec((1, gd, gd), lambda c: (c, 0, 0))],
        out_specs=pl.BlockSpec((1, gd, 2 * gd), lambda c: (c, 0, 0)),
        out_shape=jax.ShapeDtypeStruct((FNET_GROUPS, gd, 2 * gd), BF16),
        compiler_params=_params(("arbitrary",)),
        name="fnet_mix_weights",
    )(cc * ortho, sc * ortho, w_mix)

    act = jax.ShapeDtypeStruct((BATCH, NC, SEQ, CW), BF16)
    row_c = pl.BlockSpec((1, NC, TS, CW), lambda b, i: (b, 0, i, 0))
    p, q, gz = pl.pallas_call(
        _fnet_in_kernel,
        grid=(BATCH, SEQ // TS),
        in_specs=[_row_tile(D_MODEL), _resident(g.shape), _resident(w_in.shape), _resident(ab.shape)],
        out_specs=[row_c, row_c, row_c],
        out_shape=[act, act, act],
        scratch_shapes=[pltpu.VMEM((TS, D_MODEL), BF16)],
        compiler_params=_params(("arbitrary", "arbitrary")),
        name="fnet_in",
    )(x, g, w_in, ab)

    cs, ss = _dft_tables(SEQ)
    col = pl.BlockSpec((1, 1, SEQ, CW), lambda b, c: (b, c, 0, 0))
    v = pl.pallas_call(
        _fnet_seq_dft_kernel,
        grid=(BATCH, NC),
        in_specs=[_resident((SEQ, SEQ)), _resident((SEQ, SEQ)), col, col,
                  pl.BlockSpec((1, 1, CW), lambda b, c: (c, 0, 0)), col],
        out_specs=col,
        out_shape=act,
        compiler_params=_params(("arbitrary", "arbitrary")),
        name="fnet_seq_dft",
    )(cs.astype(BF16), ss.astype(BF16), p, q, _per_chunk(b_mix), gz)

    return pl.pallas_call(
        _out_proj_kernel,
        grid=(BATCH, SEQ // TS),
        in_specs=[_row_tile(D_MODEL), row_c, _resident(w_out.shape)],
        out_specs=_row_tile(D_MODEL),
        out_shape=jax.ShapeDtypeStruct((BATCH, SEQ, D_MODEL), F32),
        compiler_params=_params(("arbitrary", "arbitrary")),
        name="fnet_out",
    )(x, v, w_out)


def kernel(x, norm_g, w_out, final_g, fnet_w_in, fnet_w_mix, fnet_b_mix, conf_w_in, conf_dw_w, conf_dw_b, conf_ln_g, conf_ln_b, pool_w_in, pool_w_grp, pool_scale, sc_w_in, sc_conv_w):
    assert x.shape == (BATCH, SEQ, D_MODEL) and x.dtype == F32
    assert fnet_w_in.shape[0] == conf_w_in.shape[0] == pool_w_in.shape[0] == sc_w_in.shape[0] == 1
    g = norm_g.reshape(4, 1, D_MODEL)
    w_out_c = w_out.reshape(4, NC, CW, D_MODEL).astype(BF16)
    slabs = lambda n, rows: pltpu.VMEM((n, rows, LANES), F32)

    x = _fnet_layer(x, g[0], _chunk_major(fnet_w_in[0]), fnet_w_mix[0], fnet_b_mix[0], w_out_c[0])

    dw_w = conf_dw_w[0].reshape(CONF_KERNEL, NLT, LANES).transpose(1, 0, 2)
    x = _tiled_layer(
        _conf_kernel, "conformer_layer", x,
        [g[1], _chunk_major(conf_w_in[0]), dw_w, conf_dw_b.reshape(NLT, 1, LANES),
         _per_chunk(conf_ln_g), _per_chunk(conf_ln_b), w_out_c[1]],
        [slabs(NLT, EXT), slabs(NLT, TS)])

    x = _tiled_layer(
        _pool_kernel, "pool_layer", x,
        [g[2], _chunk_major(pool_w_in[0]), pool_w_grp[0].astype(BF16), _per_chunk(pool_scale), w_out_c[2]],
        [slabs(LT, EXT), pltpu.VMEM((TS, CW), BF16)])

    conv_w = sc_conv_w[0].reshape(SHORT_CONV_WIDTH, NC, CW).transpose(1, 0, 2)
    x = _tiled_layer(
        _sconv_kernel, "short_conv_layer", x,
        [g[3], _chunk_major(sc_w_in[0]), conv_w, w_out_c[3], final_g.reshape(1, D_MODEL)],
        [slabs(LT, EXT)])
    return x
```

```python
import jax
import jax.numpy as jnp
import numpy as np
from jax import lax
from jax.experimental import pallas as pl
from jax.experimental.pallas import tpu as pltpu

D_MODEL = 1024
D_INNER = 2048
BATCH = 8
SEQ = 2048
FNET_GROUPS = 4
FNET_GROUP_DIM = D_INNER // FNET_GROUPS
CONF_KERNEL = 31
POOL_WINDOWS = (2, 4, 8, 16)
SHORT_CONV_WIDTH = 3
RMS_EPS = 1e-6
LN_EPS = 1e-5

TS = 512
HALO = 16
EXT = TS + 2 * HALO
CW = 512
NC = D_INNER // CW
LANES = 128
SUBLANES = 8
LT = CW // LANES
NLT = D_INNER // LANES
FFT_N1 = 256
FFT_N2 = SEQ // FFT_N1
FFT_ROWS = 16
VMEM_LIMIT = 56 * 1024 * 1024

F32 = jnp.float32
BF16 = jnp.bfloat16


def _dot(a, b):
    return jnp.dot(a, b, preferred_element_type=F32)


def _rms(v, g):
    ms = jnp.mean(v * v, axis=-1, keepdims=True)
    return v * lax.rsqrt(ms + RMS_EPS) * g


def _silu(v):
    return v * jax.nn.sigmoid(v)


def _fill_xn_ext(xm_ref, xp_ref, xq_ref, g_ref, xn_s):
    i = pl.program_id(1)
    g = g_ref[...]
    prev = jnp.where(i > 0, _rms(xp_ref[0], g), 0.0)
    nxt = jnp.where(i < pl.num_programs(1) - 1, _rms(xq_ref[0], g), 0.0)
    xn_s[0:HALO, :] = prev.astype(BF16)
    xn_s[HALO:HALO + TS, :] = _rms(xm_ref[0], g).astype(BF16)
    xn_s[HALO + TS:EXT, :] = nxt.astype(BF16)


def _store_lane_tiles(slab_ref, first, val):
    for lt in range(LT):
        slab_ref[first + lt] = val[:, lt * LANES:(lt + 1) * LANES]


def _conf_kernel(xm_ref, xp_ref, xq_ref, g_ref, w_in_ref, dw_w_ref, dw_b_ref, ln_g_ref, ln_b_ref,
                 w_out_ref, o_ref, xn_s, h_s, hc_s):
    _fill_xn_ext(xm_ref, xp_ref, xq_ref, g_ref, xn_s)

    def glu_chunk(c, carry):
        xe = xn_s[...]
        a = _dot(xe, w_in_ref[c])
        gate = _dot(xe, w_in_ref[NC + c])
        _store_lane_tiles(h_s, c * LT, a * jax.nn.sigmoid(gate))
        return carry
    lax.fori_loop(0, NC, glu_chunk, 0)

    def conv_lane_tile(l, carry):
        wk = [dw_w_ref[l, k:k + 1, :] for k in range(CONF_KERNEL)]
        bias = dw_b_ref[l]
        for j in range(TS // SUBLANES):
            r0 = j * SUBLANES
            base = r0 + HALO - CONF_KERNEL // 2
            acc = h_s[l, base:base + SUBLANES, :] * wk[0]
            for k in range(1, CONF_KERNEL):
                acc = acc + h_s[l, base + k:base + k + SUBLANES, :] * wk[k]
            hc_s[l, r0:r0 + SUBLANES, :] = acc + bias
        return carry
    lax.fori_loop(0, NLT, conv_lane_tile, 0)

    tot = hc_s[0]
    for l in range(1, NLT):
        tot = tot + hc_s[l]
    mu = jnp.sum(tot, axis=-1, keepdims=True) * (1.0 / D_INNER)
    sq = jnp.square(hc_s[0] - mu)
    for l in range(1, NLT):
        sq = sq + jnp.square(hc_s[l] - mu)
    rstd = lax.rsqrt(jnp.sum(sq, axis=-1, keepdims=True) * (1.0 / D_INNER) + LN_EPS)

    o_ref[0] = xm_ref[0]

    def out_chunk(c, carry):
        hc = jnp.concatenate([hc_s[c * LT + lt] for lt in range(LT)], axis=-1)
        hn = (hc - mu) * rstd * ln_g_ref[c] + ln_b_ref[c]
        z = _dot(xn_s[HALO:HALO + TS, :], w_in_ref[2 * NC + c])
        v = (_silu(hn) * _silu(z)).astype(BF16)
        o_ref[0] += _dot(v, w_out_ref[c])
        return carry
    lax.fori_loop(0, NC, out_chunk, 0)


def _pool_kernel(xm_ref, xp_ref, xq_ref, g_ref, w_in_ref, w_grp_ref, scale_ref, w_out_ref, o_ref,
                 xn_s, u_s, p_s):
    _fill_xn_ext(xm_ref, xp_ref, xq_ref, g_ref, xn_s)
    t = pl.program_id(1) * TS + lax.broadcasted_iota(jnp.int32, (TS, 1), 0)
    o_ref[0] = xm_ref[0]

    def chunk(c, carry):
        _store_lane_tiles(u_s, 0, _dot(xn_s[...], w_in_ref[c]))
        for grp, w in enumerate(POOL_WINDOWS):
            left = w // 2
            right = w - 1 - left

            @pl.when(c == grp)
            def _():
                cnt = jnp.minimum(t + right + 1, SEQ) - jnp.maximum(t - left, 0)
                inv_cnt = 1.0 / cnt.astype(F32)
                for lt in range(LT):
                    wsum = u_s[lt, HALO - left:HALO - left + TS, :]
                    for k in range(1, w):
                        wsum = wsum + u_s[lt, HALO - left + k:HALO - left + k + TS, :]
                    p_s[:, lt * LANES:(lt + 1) * LANES] = (
                        wsum * inv_cnt - u_s[lt, HALO:HALO + TS, :]).astype(BF16)
        y = _dot(p_s[...], w_grp_ref[c]) * scale_ref[c]
        z = _dot(xn_s[HALO:HALO + TS, :], w_in_ref[NC + c])
        v = (y * _silu(z)).astype(BF16)
        o_ref[0] += _dot(v, w_out_ref[c])
        return carry
    lax.fori_loop(0, NC, chunk, 0)


def _sconv_kernel(xm_ref, xp_ref, xq_ref, g_ref, w_in_ref, conv_w_ref, w_out_ref, fg_ref, o_ref,
                  xn_s, ch_s):
    _fill_xn_ext(xm_ref, xp_ref, xq_ref, g_ref, xn_s)
    o_ref[0] = xm_ref[0]

    def chunk(c, carry):
        xe = xn_s[...]
        cg = _dot(xe, w_in_ref[NC + c])
        hh = _dot(xe, w_in_ref[2 * NC + c])
        _store_lane_tiles(ch_s, 0, cg * hh)
        cw = conv_w_ref[c]
        base = HALO - SHORT_CONV_WIDTH // 2
        parts = []
        for lt in range(LT):
            ls = slice(lt * LANES, (lt + 1) * LANES)
            conv = ch_s[lt, base:base + TS, :] * cw[0:1, ls]
            for k in range(1, SHORT_CONV_WIDTH):
                conv = conv + ch_s[lt, base + k:base + k + TS, :] * cw[k:k + 1, ls]
            parts.append(conv)
        conv = jnp.concatenate(parts, axis=-1)
        xm = xn_s[HALO:HALO + TS, :]
        bg = _dot(xm, w_in_ref[c])
        z = _dot(xm, w_in_ref[3 * NC + c])
        v = (bg * conv * _silu(z)).astype(BF16)
        o_ref[0] += _dot(v, w_out_ref[c])
        return carry
    lax.fori_loop(0, NC, chunk, 0)
    o_ref[0] = _rms(o_ref[0], fg_ref[...])


def _fnet_mix_weights_kernel(cc_ref, sc_ref, w_ref, o_ref):
    w = w_ref[0]
    gd = FNET_GROUP_DIM
    o_ref[0, :, 0:gd] = jnp.dot(cc_ref[...], w, preferred_element_type=F32,
                                precision=lax.Precision.HIGHEST).astype(BF16)
    o_ref[0, :, gd:2 * gd] = jnp.dot(sc_ref[...], w, preferred_element_type=F32,
                                     precision=lax.Precision.HIGHEST).astype(BF16)


def _fnet_in_kernel(x_ref, g_ref, w_in_ref, ab_ref, pq_ref, gz_ref, xn_s):
    xn_s[...] = _rms(x_ref[0], g_ref[...]).astype(BF16)
    gd = FNET_GROUP_DIM

    def chunk(c, carry):
        xn = xn_s[...]
        u = _dot(xn, w_in_ref[c]).astype(BF16)
        pq = _dot(u, ab_ref[c])
        pq_ref[0, c, 0] = pq[:, 0:gd].astype(BF16)
        pq_ref[0, c, 1] = pq[:, gd:2 * gd].astype(BF16)
        gz_ref[0, c] = _silu(_dot(xn, w_in_ref[NC + c])).astype(BF16)
        return carry
    lax.fori_loop(0, NC, chunk, 0)


def _fnet_seq_dft_kernel(m_ref, twc_ref, tws_ref, pq_ref, b_ref, gz_ref, v_ref, a_s):
    for n2 in range(FFT_N2):
        a_s[n2] = _dot(m_ref[...], pq_ref[0, 0, :, n2 * CW:(n2 + 1) * CW])

    rt = 0.5 ** 0.5

    def row_block(rb, carry):
        r0 = pl.multiple_of(rb * FFT_ROWS, FFT_ROWS)
        re_rows = pl.ds(r0, FFT_ROWS)
        im_rows = pl.ds(FFT_N1 + r0, FFT_ROWS)
        for lt in range(LT):
            ls = slice(lt * LANES, (lt + 1) * LANES)
            a = [a_s[0, re_rows, ls]]
            b = [None]
            for n2 in range(1, FFT_N2):
                ar = a_s[n2, re_rows, ls]
                ai = a_s[n2, im_rows, ls]
                tc = twc_ref[n2, re_rows, :]
                ts = tws_ref[n2, re_rows, :]
                a.append(ar * tc + ai * ts)
                b.append(None if n2 == FFT_N2 // 2 else ai * tc - ar * ts)
            s04, d04 = a[0] + a[4], a[0] - a[4]
            p1, p2, p3 = a[1] + a[7], a[2] + a[6], a[3] + a[5]
            p13, t = p1 + p3, rt * (p1 - p3)
            e = s04 + p2
            q1, q2, q3 = b[1] - b[7], b[2] - b[6], b[3] - b[5]
            u = rt * (q1 + q3)
            ca = [e + p13, d04 + t, s04 - p2, d04 - t, e - p13]
            sb = [None, q2 + u, q1 - q3, u - q2]
            out = [ca[0], ca[1] + sb[1], ca[2] + sb[2], ca[3] + sb[3], ca[4],
                   ca[3] - sb[3], ca[2] - sb[2], ca[1] - sb[1]]
            bias = b_ref[0, :, ls]
            for k2 in range(FFT_N2):
                rows = pl.ds(k2 * FFT_N1 + r0, FFT_ROWS)
                v_ref[0, 0, rows, ls] = ((out[k2] + bias) * gz_ref[0, 0, rows, ls].astype(F32)).astype(BF16)
        return carry
    lax.fori_loop(0, FFT_N1 // FFT_ROWS, row_block, 0)


def _out_proj_kernel(x_ref, v_ref, w_out_ref, o_ref):
    o_ref[0] = x_ref[0]

    def chunk(c, carry):
        o_ref[0] += _dot(v_ref[0, c], w_out_ref[c])
        return carry
    lax.fori_loop(0, NC, chunk, 0)


def _resident(shape):
    nd = len(shape)
    return pl.BlockSpec(shape, lambda *_: (0,) * nd, pipeline_mode=pl.Buffered(1))


def _params(sem):
    return pltpu.CompilerParams(dimension_semantics=sem, vmem_limit_bytes=VMEM_LIMIT)


def _row_tile(width):
    return pl.BlockSpec((1, TS, width), lambda b, i: (b, i, 0))


def _tiled_layer(body, name, x, consts, scratch):
    per_tile = TS // HALO
    prev = pl.BlockSpec((1, HALO, D_MODEL), lambda b, i: (b, jnp.maximum(i * per_tile - 1, 0), 0))
    nxt = pl.BlockSpec((1, HALO, D_MODEL),
                       lambda b, i: (b, jnp.minimum((i + 1) * per_tile, SEQ // HALO - 1), 0))
    return pl.pallas_call(
        body,
        grid=(BATCH, SEQ // TS),
        in_specs=[_row_tile(D_MODEL), prev, nxt] + [_resident(c.shape) for c in consts],
        out_specs=_row_tile(D_MODEL),
        out_shape=jax.ShapeDtypeStruct((BATCH, SEQ, D_MODEL), F32),
        scratch_shapes=[pltpu.VMEM((EXT, D_MODEL), BF16)] + scratch,
        compiler_params=_params(("arbitrary", "arbitrary")),
        name=name,
    )(x, x, x, *consts)


def _chunk_major(w):
    return w.reshape(D_MODEL, -1, CW).transpose(1, 0, 2).astype(BF16)


def _per_chunk(v):
    return v.reshape(NC, 1, CW)


def _dft_tables(n):
    k = np.arange(n, dtype=np.int64)
    ang = ((k[:, None] * k[None, :]) % n) * (2.0 * np.pi / n)
    return np.cos(ang), np.sin(ang)


def _seq_fft_tables():
    c, s = _dft_tables(FFT_N1)
    m = np.block([[c, -s], [-s, -c]])
    n2 = np.arange(FFT_N2, dtype=np.int64)[:, None]
    k1 = np.arange(FFT_N1, dtype=np.int64)[None, :]
    ang = ((n2 * k1) % SEQ) * (2.0 * np.pi / SEQ)
    bcast = lambda t: np.broadcast_to(t[:, :, None], (FFT_N2, FFT_N1, LANES))
    return (jnp.asarray(m, dtype=BF16), jnp.asarray(bcast(np.cos(ang)), dtype=F32),
            jnp.asarray(bcast(np.sin(ang)), dtype=F32))


def _fnet_layer(x, g, w_in, w_mix, b_mix, w_out):
    gd = FNET_GROUP_DIM
    ortho = 1.0 / (SEQ * gd) ** 0.5
    cc, sc = _dft_tables(gd)
    ab = pl.pallas_call(
        _fnet_mix_weights_kernel,
        grid=(FNET_GROUPS,),
        in_specs=[_resident((gd, gd)), _resident((gd, gd)),
                  pl.BlockSpec((1, gd, gd), lambda c: (c, 0, 0))],
        out_specs=pl.BlockSpec((1, gd, 2 * gd), lambda c: (c, 0, 0)),
        out_shape=jax.ShapeDtypeStruct((FNET_GROUPS, gd, 2 * gd), BF16),
        compiler_params=_params(("arbitrary",)),
        name="fnet_mix_weights",
    )(jnp.asarray(cc * ortho, dtype=F32), jnp.asarray(sc * ortho, dtype=F32), w_mix)

    act = jax.ShapeDtypeStruct((BATCH, NC, SEQ, CW), BF16)
    row_c = pl.BlockSpec((1, NC, TS, CW), lambda b, i: (b, 0, i, 0))
    pq, gz = pl.pallas_call(
        _fnet_in_kernel,
        grid=(BATCH, SEQ // TS),
        in_specs=[_row_tile(D_MODEL), _resident(g.shape), _resident(w_in.shape), _resident(ab.shape)],
        out_specs=[pl.BlockSpec((1, NC, 2, TS, CW), lambda b, i: (b, 0, 0, i, 0)), row_c],
        out_shape=[jax.ShapeDtypeStruct((BATCH, NC, 2, SEQ, CW), BF16), act],
        scratch_shapes=[pltpu.VMEM((TS, D_MODEL), BF16)],
        compiler_params=_params(("arbitrary", "arbitrary")),
        name="fnet_in",
    )(x, g, w_in, ab)

    pq = pq.reshape(BATCH, NC, 2 * FFT_N1, FFT_N2 * CW)
    m, twc, tws = _seq_fft_tables()
    col = pl.BlockSpec((1, 1, SEQ, CW), lambda b, c: (b, c, 0, 0))
    v = pl.pallas_call(
        _fnet_seq_dft_kernel,
        grid=(BATCH, NC),
        in_specs=[_resident(m.shape), _resident(twc.shape), _resident(tws.shape),
                  pl.BlockSpec((1, 1, 2 * FFT_N1, FFT_N2 * CW), lambda b, c: (b, c, 0, 0)),
                  pl.BlockSpec((1, 1, CW), lambda b, c: (c, 0, 0)), col],
        out_specs=col,
        out_shape=act,
        scratch_shapes=[pltpu.VMEM((FFT_N2, 2 * FFT_N1, CW), F32)],
        compiler_params=_params(("arbitrary", "arbitrary")),
        name="fnet_seq_dft",
    )(m, twc, tws, pq, _per_chunk(b_mix), gz)

    return pl.pallas_call(
        _out_proj_kernel,
        grid=(BATCH, SEQ // TS),
        in_specs=[_row_tile(D_MODEL), row_c, _resident(w_out.shape)],
        out_specs=_row_tile(D_MODEL),
        out_shape=jax.ShapeDtypeStruct((BATCH, SEQ, D_MODEL), F32),
        compiler_params=_params(("arbitrary", "arbitrary")),
        name="fnet_out",
    )(x, v, w_out)


def kernel(x, norm_g, w_out, final_g, fnet_w_in, fnet_w_mix, fnet_b_mix, conf_w_in, conf_dw_w, conf_dw_b, conf_ln_g, conf_ln_b, pool_w_in, pool_w_grp, pool_scale, sc_w_in, sc_conv_w):
    assert x.shape == (BATCH, SEQ, D_MODEL) and x.dtype == F32
    assert fnet_w_in.shape[0] == conf_w_in.shape[0] == pool_w_in.shape[0] == sc_w_in.shape[0] == 1
    g = norm_g.reshape(4, 1, D_MODEL)
    w_out_c = w_out.reshape(4, NC, CW, D_MODEL).astype(BF16)
    slabs = lambda n, rows: pltpu.VMEM((n, rows, LANES), F32)

    x = _fnet_layer(x, g[0], _chunk_major(fnet_w_in[0]), fnet_w_mix[0], fnet_b_mix[0], w_out_c[0])

    dw_w = conf_dw_w[0].reshape(CONF_KERNEL, NLT, LANES).transpose(1, 0, 2)
    x = _tiled_layer(
        _conf_kernel, "conformer_layer", x,
        [g[1], _chunk_major(conf_w_in[0]), dw_w, conf_dw_b.reshape(NLT, 1, LANES),
         _per_chunk(conf_ln_g), _per_chunk(conf_ln_b), w_out_c[1]],
        [slabs(NLT, EXT), slabs(NLT, TS)])

    x = _tiled_layer(
        _pool_kernel, "pool_layer", x,
        [g[2], _chunk_major(pool_w_in[0]), pool_w_grp[0].astype(BF16), _per_chunk(pool_scale), w_out_c[2]],
        [slabs(LT, EXT), pltpu.VMEM((TS, CW), BF16)])

    conv_w = sc_conv_w[0].reshape(SHORT_CONV_WIDTH, NC, CW).transpose(1, 0, 2)
    x = _tiled_layer(
        _sconv_kernel, "short_conv_layer", x,
        [g[3], _chunk_major(sc_w_in[0]), conv_w, w_out_c[3], final_g.reshape(1, D_MODEL)],
        [slabs(LT, EXT)])
    return x
```

```python
import jax
import jax.numpy as jnp
import numpy as np
from jax import lax
from jax.experimental import pallas as pl
from jax.experimental.pallas import tpu as pltpu

D_MODEL = 1024
D_INNER = 2048
BATCH = 8
SEQ = 2048
FNET_GROUPS = 4
FNET_GROUP_DIM = D_INNER // FNET_GROUPS
CONF_KERNEL = 31
POOL_WINDOWS = (2, 4, 8, 16)
SHORT_CONV_WIDTH = 3
RMS_EPS = 1e-6
LN_EPS = 1e-5

TS = 512
HALO = 16
EXT = TS + 2 * HALO
CW = 512
NC = D_INNER // CW
LANES = 128
SUBLANES = 8
LT = CW // LANES
NLT = D_INNER // LANES
FFT_N1 = 256
FFT_N2 = SEQ // FFT_N1
FFT_ROWS = 16
FFT_HALF = 256
CHUNK_UNROLL = 4
VMEM_LIMIT = 56 * 1024 * 1024

F32 = jnp.float32
BF16 = jnp.bfloat16


def _dot(a, b):
    return jnp.dot(a, b, preferred_element_type=F32)


def _rms(v, g):
    ms = jnp.mean(v * v, axis=-1, keepdims=True)
    return v * lax.rsqrt(ms + RMS_EPS) * g


def _silu(v):
    return v * jax.nn.sigmoid(v)


def _fill_xn_ext(xm_ref, xp_ref, xq_ref, g_ref, xn_s):
    i = pl.program_id(1)
    g = g_ref[...]
    prev = jnp.where(i > 0, _rms(xp_ref[0], g), 0.0)
    nxt = jnp.where(i < pl.num_programs(1) - 1, _rms(xq_ref[0], g), 0.0)
    xn_s[0:HALO, :] = prev.astype(BF16)
    xn_s[HALO:HALO + TS, :] = _rms(xm_ref[0], g).astype(BF16)
    xn_s[HALO + TS:EXT, :] = nxt.astype(BF16)


def _store_lane_tiles(slab_ref, first, val):
    for lt in range(LT):
        slab_ref[first + lt] = val[:, lt * LANES:(lt + 1) * LANES]


def _conf_filter_kernel(trig_ref, w_ref, o_ref):
    o_ref[0] = jnp.dot(trig_ref[...], w_ref[...], preferred_element_type=F32,
                       precision=lax.Precision.HIGHEST)


def _conf_kernel(xm_ref, xp_ref, xq_ref, g_ref, w_in_ref, fwd_ref, inv_ref, gtab_ref, dw_b_ref,
                 ln_g_ref, ln_b_ref, w_out_ref, o_ref, xn_s, hc_s):
    _fill_xn_ext(xm_ref, xp_ref, xq_ref, g_ref, xn_s)
    hw = EXT // 2

    def conv_chunk(c, carry):
        xe = xn_s[...]
        a = _dot(xe, w_in_ref[c])
        gate = _dot(xe, w_in_ref[NC + c])
        h = (a * jax.nn.sigmoid(gate)).astype(BF16)
        spec = _dot(fwd_ref[...], h)
        top, bot = spec[0:hw], spec[hw:EXT]
        g_re_top = gtab_ref[c, 0:hw]
        g_re_bot = gtab_ref[c, hw:EXT]
        g_im = gtab_ref[c, EXT:EXT + hw]
        y = jnp.concatenate([top * g_re_top - bot * g_im, top * g_im + bot * g_re_bot], axis=0)
        hc_s[c] = _dot(inv_ref[...], y.astype(BF16)) + dw_b_ref[c]
        return carry
    lax.fori_loop(0, NC, conv_chunk, 0, unroll=CHUNK_UNROLL)

    tot = jnp.sum(hc_s[0], axis=-1, keepdims=True)
    for c in range(1, NC):
        tot = tot + jnp.sum(hc_s[c], axis=-1, keepdims=True)
    mu = tot * (1.0 / D_INNER)
    sq = jnp.sum(jnp.square(hc_s[0] - mu), axis=-1, keepdims=True)
    for c in range(1, NC):
        sq = sq + jnp.sum(jnp.square(hc_s[c] - mu), axis=-1, keepdims=True)
    rstd = lax.rsqrt(sq * (1.0 / D_INNER) + LN_EPS)

    o_ref[0] = xm_ref[0]

    def out_chunk(c, carry):
        hn = (hc_s[c] - mu) * rstd * ln_g_ref[c] + ln_b_ref[c]
        z = _dot(xn_s[HALO:HALO + TS, :], w_in_ref[2 * NC + c])
        v = (_silu(hn) * _silu(z)).astype(BF16)
        o_ref[0] += _dot(v, w_out_ref[c])
        return carry
    lax.fori_loop(0, NC, out_chunk, 0, unroll=CHUNK_UNROLL)


def _pool_kernel(xm_ref, xp_ref, xq_ref, g_ref, w_in_ref, w_grp_ref, scale_ref, w_out_ref, o_ref,
                 xn_s, u_s, p_s):
    _fill_xn_ext(xm_ref, xp_ref, xq_ref, g_ref, xn_s)
    t = pl.program_id(1) * TS + lax.broadcasted_iota(jnp.int32, (TS, 1), 0)
    o_ref[0] = xm_ref[0]

    def chunk(c, carry):
        _store_lane_tiles(u_s, 0, _dot(xn_s[...], w_in_ref[c]))
        for grp, w in enumerate(POOL_WINDOWS):
            left = w // 2
            right = w - 1 - left

            @pl.when(c == grp)
            def _():
                cnt = jnp.minimum(t + right + 1, SEQ) - jnp.maximum(t - left, 0)
                inv_cnt = 1.0 / cnt.astype(F32)
                for lt in range(LT):
                    wsum = u_s[lt, HALO - left:HALO - left + TS, :]
                    for k in range(1, w):
                        wsum = wsum + u_s[lt, HALO - left + k:HALO - left + k + TS, :]
                    p_s[:, lt * LANES:(lt + 1) * LANES] = (
                        wsum * inv_cnt - u_s[lt, HALO:HALO + TS, :]).astype(BF16)
        y = _dot(p_s[...], w_grp_ref[c]) * scale_ref[c]
        z = _dot(xn_s[HALO:HALO + TS, :], w_in_ref[NC + c])
        v = (y * _silu(z)).astype(BF16)
        o_ref[0] += _dot(v, w_out_ref[c])
        return carry
    lax.fori_loop(0, NC, chunk, 0, unroll=CHUNK_UNROLL)


def _sconv_kernel(xm_ref, xp_ref, xq_ref, g_ref, w_in_ref, conv_w_ref, w_out_ref, fg_ref, o_ref,
                  xn_s, ch_s):
    _fill_xn_ext(xm_ref, xp_ref, xq_ref, g_ref, xn_s)
    o_ref[0] = xm_ref[0]

    def chunk(c, carry):
        xe = xn_s[...]
        cg = _dot(xe, w_in_ref[NC + c])
        hh = _dot(xe, w_in_ref[2 * NC + c])
        _store_lane_tiles(ch_s, 0, cg * hh)
        cw = conv_w_ref[c]
        base = HALO - SHORT_CONV_WIDTH // 2
        parts = []
        for lt in range(LT):
            ls = slice(lt * LANES, (lt + 1) * LANES)
            conv = ch_s[lt, base:base + TS, :] * cw[0:1, ls]
            for k in range(1, SHORT_CONV_WIDTH):
                conv = conv + ch_s[lt, base + k:base + k + TS, :] * cw[k:k + 1, ls]
            parts.append(conv)
        conv = jnp.concatenate(parts, axis=-1)
        xm = xn_s[HALO:HALO + TS, :]
        bg = _dot(xm, w_in_ref[c])
        z = _dot(xm, w_in_ref[3 * NC + c])
        v = (bg * conv * _silu(z)).astype(BF16)
        o_ref[0] += _dot(v, w_out_ref[c])
        return carry
    lax.fori_loop(0, NC, chunk, 0, unroll=CHUNK_UNROLL)
    o_ref[0] = _rms(o_ref[0], fg_ref[...])


def _fnet_mix_weights_kernel(cc_ref, sc_ref, w_ref, o_ref):
    w = w_ref[0]
    gd = FNET_GROUP_DIM
    o_ref[0, :, 0:gd] = jnp.dot(cc_ref[...], w, preferred_element_type=F32,
                                precision=lax.Precision.HIGHEST).astype(BF16)
    o_ref[0, :, gd:2 * gd] = jnp.dot(sc_ref[...], w, preferred_element_type=F32,
                                     precision=lax.Precision.HIGHEST).astype(BF16)


def _fnet_in_kernel(x_ref, g_ref, w_in_ref, ab_ref, pq_ref, xn_s, pq_s):
    xn_s[...] = _rms(x_ref[0], g_ref[...]).astype(BF16)
    n1_rows = TS // FFT_N2

    def chunk(c, carry):
        u = _dot(xn_s[...], w_in_ref[c]).astype(BF16)
        pq = _dot(u, ab_ref[c])
        for lt in range(2 * LT):
            pq_s[lt] = pq[:, lt * LANES:(lt + 1) * LANES]
        for n2 in range(FFT_N2):
            for part in range(2):
                for lt in range(LT):
                    piece = pq_s[part * LT + lt, pl.ds(n2, n1_rows, stride=FFT_N2), :]
                    pq_ref[0, c, n2, part, :, lt * LANES:(lt + 1) * LANES] = piece.astype(BF16)
        return carry
    lax.fori_loop(0, NC, chunk, 0, unroll=CHUNK_UNROLL)


def _fnet_seq_dft_kernel(m_ref, twc_ref, tws_ref, pq_ref, b_ref, f_ref, a_s):
    rt = 0.5 ** 0.5

    def mxu_stage(half):
        hl = slice(half * FFT_HALF, (half + 1) * FFT_HALF)
        for n2 in range(FFT_N2):
            a_s[n2, :, hl] = _dot(m_ref[...], pq_ref[0, 0, n2, :, :, hl].reshape(2 * FFT_N1, FFT_HALF))

    def row_block(half, rb):
        r0 = rb * FFT_ROWS
        re_rows = pl.ds(r0, FFT_ROWS)
        im_rows = pl.ds(FFT_N1 + r0, FFT_ROWS)
        for lt in range(half * FFT_HALF // LANES, (half + 1) * FFT_HALF // LANES):
            ls = slice(lt * LANES, (lt + 1) * LANES)
            a = [a_s[0, re_rows, ls]]
            b = [None]
            for n2 in range(1, FFT_N2):
                ar = a_s[n2, re_rows, ls]
                ai = a_s[n2, im_rows, ls]
                tc = twc_ref[n2, re_rows, :]
                ts = tws_ref[n2, re_rows, :]
                a.append(ar * tc + ai * ts)
                b.append(None if n2 == FFT_N2 // 2 else ai * tc - ar * ts)
            s04, d04 = a[0] + a[4], a[0] - a[4]
            p1, p2, p3 = a[1] + a[7], a[2] + a[6], a[3] + a[5]
            p13, t = p1 + p3, rt * (p1 - p3)
            e = s04 + p2
            q1, q2, q3 = b[1] - b[7], b[2] - b[6], b[3] - b[5]
            u = rt * (q1 + q3)
            ca = [e + p13, d04 + t, s04 - p2, d04 - t, e - p13]
            sb = [None, q2 + u, q1 - q3, u - q2]
            out = [ca[0], ca[1] + sb[1], ca[2] + sb[2], ca[3] + sb[3], ca[4],
                   ca[3] - sb[3], ca[2] - sb[2], ca[1] - sb[1]]
            bias = b_ref[0, :, ls]
            for k2 in range(FFT_N2):
                rows = pl.ds(k2 * FFT_N1 + r0, FFT_ROWS)
                f_ref[0, 0, rows, ls] = (out[k2] + bias).astype(BF16)

    for half in range(CW // FFT_HALF):
        mxu_stage(half)
        for rb in range(FFT_N1 // FFT_ROWS):
            row_block(half, rb)


def _fnet_out_kernel(x_ref, g_ref, w_in_ref, f_ref, w_out_ref, o_ref, xn_s):
    xn_s[...] = _rms(x_ref[0], g_ref[...]).astype(BF16)
    o_ref[0] = x_ref[0]

    def chunk(c, carry):
        z = _dot(xn_s[...], w_in_ref[NC + c])
        v = (f_ref[0, c].astype(F32) * _silu(z)).astype(BF16)
        o_ref[0] += _dot(v, w_out_ref[c])
        return carry
    lax.fori_loop(0, NC, chunk, 0, unroll=CHUNK_UNROLL)


def _resident(shape):
    nd = len(shape)
    return pl.BlockSpec(shape, lambda *_: (0,) * nd, pipeline_mode=pl.Buffered(1))


def _params(sem):
    return pltpu.CompilerParams(dimension_semantics=sem, vmem_limit_bytes=VMEM_LIMIT)


def _row_tile(width):
    return pl.BlockSpec((1, TS, width), lambda b, i: (b, i, 0))


def _tiled_layer(body, name, x, consts, scratch):
    per_tile = TS // HALO
    prev = pl.BlockSpec((1, HALO, D_MODEL), lambda b, i: (b, jnp.maximum(i * per_tile - 1, 0), 0))
    nxt = pl.BlockSpec((1, HALO, D_MODEL),
                       lambda b, i: (b, jnp.minimum((i + 1) * per_tile, SEQ // HALO - 1), 0))
    return pl.pallas_call(
        body,
        grid=(BATCH, SEQ // TS),
        in_specs=[_row_tile(D_MODEL), prev, nxt] + [_resident(c.shape) for c in consts],
        out_specs=_row_tile(D_MODEL),
        out_shape=jax.ShapeDtypeStruct((BATCH, SEQ, D_MODEL), F32),
        scratch_shapes=[pltpu.VMEM((EXT, D_MODEL), BF16)] + scratch,
        compiler_params=_params(("arbitrary", "arbitrary")),
        name=name,
    )(x, x, x, *consts)


def _chunk_major(w):
    return w.reshape(D_MODEL, -1, CW).transpose(1, 0, 2).astype(BF16)


def _per_chunk(v):
    return v.reshape(NC, 1, CW)


def _dft_tables(n):
    k = np.arange(n, dtype=np.int64)
    ang = ((k[:, None] * k[None, :]) % n) * (2.0 * np.pi / n)
    return np.cos(ang), np.sin(ang)


def _seq_fft_tables():
    c, s = _dft_tables(FFT_N1)
    m = np.block([[c, -s], [-s, -c]])
    n2 = np.arange(FFT_N2, dtype=np.int64)[:, None]
    k1 = np.arange(FFT_N1, dtype=np.int64)[None, :]
    ang = ((n2 * k1) % SEQ) * (2.0 * np.pi / SEQ)
    bcast = lambda t: np.broadcast_to(t[:, :, None], (FFT_N2, FFT_N1, LANES))
    return (jnp.asarray(m, dtype=BF16), jnp.asarray(bcast(np.cos(ang)), dtype=F32),
            jnp.asarray(bcast(np.sin(ang)), dtype=F32))


def _conv_dft_tables():
    n, hw = EXT, EXT // 2
    t = np.arange(n, dtype=np.int64)
    f = np.arange(hw, dtype=np.int64)
    ang = ((f[:, None] * t[None, :]) % n) * (2.0 * np.pi / n)
    fwd = np.concatenate([np.cos(ang), -np.sin(ang)], axis=0)
    fwd[hw] = np.cos(np.pi * t)
    te = np.arange(HALO, HALO + TS, dtype=np.int64)
    angi = ((te[:, None] * f[None, :]) % n) * (2.0 * np.pi / n)
    inv = np.concatenate([np.cos(angi), -np.sin(angi)], axis=1) * (2.0 / n)
    inv[:, 0] = 1.0 / n
    inv[:, hw] = np.cos(np.pi * te) / n
    lag = CONF_KERNEL // 2 - np.arange(CONF_KERNEL, dtype=np.int64)
    angf = ((f[:, None] * lag[None, :]) % n) * (2.0 * np.pi / n)
    re_bot = np.cos(angf)
    re_bot[0] = np.cos(np.pi * lag)
    trig = np.zeros((n + hw, LANES))
    trig[:, :CONF_KERNEL] = np.concatenate([np.cos(angf), re_bot, -np.sin(angf)], axis=0)
    return jnp.asarray(fwd, dtype=BF16), jnp.asarray(inv, dtype=BF16), jnp.asarray(trig, dtype=F32)


def _fnet_layer(x, g, w_in, w_mix, b_mix, w_out):
    gd = FNET_GROUP_DIM
    ortho = 1.0 / (SEQ * gd) ** 0.5
    cc, sc = _dft_tables(gd)
    ab = pl.pallas_call(
        _fnet_mix_weights_kernel,
        grid=(FNET_GROUPS,),
        in_specs=[_resident((gd, gd)), _resident((gd, gd)),
                  pl.BlockSpec((1, gd, gd), lambda c: (c, 0, 0))],
        out_specs=pl.BlockSpec((1, gd, 2 * gd), lambda c: (c, 0, 0)),
        out_shape=jax.ShapeDtypeStruct((FNET_GROUPS, gd, 2 * gd), BF16),
        compiler_params=_params(("arbitrary",)),
        name="fnet_mix_weights",
    )(jnp.asarray(cc * ortho, dtype=F32), jnp.asarray(sc * ortho, dtype=F32), w_mix)

    pq = pl.pallas_call(
        _fnet_in_kernel,
        grid=(BATCH, SEQ // TS),
        in_specs=[_row_tile(D_MODEL), _resident(g.shape), _resident(w_in.shape), _resident(ab.shape)],
        out_specs=pl.BlockSpec((1, NC, FFT_N2, 2, TS // FFT_N2, CW), lambda b, i: (b, 0, 0, 0, i, 0)),
        out_shape=jax.ShapeDtypeStruct((BATCH, NC, FFT_N2, 2, FFT_N1, CW), BF16),
        scratch_shapes=[pltpu.VMEM((TS, D_MODEL), BF16), pltpu.VMEM((2 * LT, TS, LANES), F32)],
        compiler_params=_params(("arbitrary", "arbitrary")),
        name="fnet_in",
    )(x, g, w_in, ab)

    m, twc, tws = _seq_fft_tables()
    f = pl.pallas_call(
        _fnet_seq_dft_kernel,
        grid=(BATCH, NC),
        in_specs=[_resident(m.shape), _resident(twc.shape), _resident(tws.shape),
                  pl.BlockSpec((1, 1, FFT_N2, 2, FFT_N1, CW), lambda b, c: (b, c, 0, 0, 0, 0)),
                  pl.BlockSpec((1, 1, CW), lambda b, c: (c, 0, 0))],
        out_specs=pl.BlockSpec((1, 1, SEQ, CW), lambda b, c: (b, c, 0, 0)),
        out_shape=jax.ShapeDtypeStruct((BATCH, NC, SEQ, CW), BF16),
        scratch_shapes=[pltpu.VMEM((FFT_N2, 2 * FFT_N1, CW), F32)],
        compiler_params=_params(("arbitrary", "arbitrary")),
        name="fnet_seq_dft",
    )(m, twc, tws, pq, _per_chunk(b_mix))

    return pl.pallas_call(
        _fnet_out_kernel,
        grid=(BATCH, SEQ // TS),
        in_specs=[_row_tile(D_MODEL), _resident(g.shape), _resident(w_in.shape),
                  pl.BlockSpec((1, NC, TS, CW), lambda b, i: (b, 0, i, 0)), _resident(w_out.shape)],
        out_specs=_row_tile(D_MODEL),
        out_shape=jax.ShapeDtypeStruct((BATCH, SEQ, D_MODEL), F32),
        scratch_shapes=[pltpu.VMEM((TS, D_MODEL), BF16)],
        compiler_params=_params(("arbitrary", "arbitrary")),
        name="fnet_out",
    )(x, g, w_in, f, w_out)


def kernel(x, norm_g, w_out, final_g, fnet_w_in, fnet_w_mix, fnet_b_mix, conf_w_in, conf_dw_w, conf_dw_b, conf_ln_g, conf_ln_b, pool_w_in, pool_w_grp, pool_scale, sc_w_in, sc_conv_w):
    assert x.shape == (BATCH, SEQ, D_MODEL) and x.dtype == F32
    assert fnet_w_in.shape[0] == conf_w_in.shape[0] == pool_w_in.shape[0] == sc_w_in.shape[0] == 1
    g = norm_g.reshape(4, 1, D_MODEL)
    w_out_c = w_out.reshape(4, NC, CW, D_MODEL).astype(BF16)
    slabs = lambda n, rows: pltpu.VMEM((n, rows, LANES), F32)

    x = _fnet_layer(x, g[0], _chunk_major(fnet_w_in[0]), fnet_w_mix[0], fnet_b_mix[0], w_out_c[0])

    fwd, inv, trig = _conv_dft_tables()
    dw_w = jnp.pad(conf_dw_w[0], ((0, LANES - CONF_KERNEL), (0, 0)))
    gtab = pl.pallas_call(
        _conf_filter_kernel,
        grid=(NC,),
        in_specs=[_resident(trig.shape), pl.BlockSpec((LANES, CW), lambda c: (0, c))],
        out_specs=pl.BlockSpec((1, EXT + EXT // 2, CW), lambda c: (c, 0, 0)),
        out_shape=jax.ShapeDtypeStruct((NC, EXT + EXT // 2, CW), F32),
        compiler_params=_params(("arbitrary",)),
        name="conformer_filter_spectrum",
    )(trig, dw_w)
    x = _tiled_layer(
        _conf_kernel, "conformer_layer", x,
        [g[1], _chunk_major(conf_w_in[0]), fwd, inv, gtab, _per_chunk(conf_dw_b),
         _per_chunk(conf_ln_g), _per_chunk(conf_ln_b), w_out_c[1]],
        [pltpu.VMEM((NC, TS, CW), F32)])

    x = _tiled_layer(
        _pool_kernel, "pool_layer", x,
        [g[2], _chunk_major(pool_w_in[0]), pool_w_grp[0].astype(BF16), _per_chunk(pool_scale), w_out_c[2]],
        [slabs(LT, EXT), pltpu.VMEM((TS, CW), BF16)])

    conv_w = sc_conv_w[0].reshape(SHORT_CONV_WIDTH, NC, CW).transpose(1, 0, 2)
    x = _tiled_layer(
        _sconv_kernel, "short_conv_layer", x,
        [g[3], _chunk_major(sc_w_in[0]), conv_w, w_out_c[3], final_g.reshape(1, D_MODEL)],
        [slabs(LT, EXT)])
    return x
```

```python
import jax
import jax.numpy as jnp
import numpy as np
from jax import lax
from jax.experimental import pallas as pl
from jax.experimental.pallas import tpu as pltpu

D_MODEL = 1024
D_INNER = 2048
BATCH = 8
SEQ = 2048
FNET_GROUPS = 4
FNET_GROUP_DIM = D_INNER // FNET_GROUPS
CONF_KERNEL = 31
POOL_WINDOWS = (2, 4, 8, 16)
SHORT_CONV_WIDTH = 3
RMS_EPS = 1e-6
LN_EPS = 1e-5

TS = 512
HALO = 16
EXT = TS + 2 * HALO
CW = 512
NC = D_INNER // CW
LANES = 128
LT = CW // LANES
FFT_N1 = 256
FFT_N2 = SEQ // FFT_N1
FFT_ROWS = 16
FFT_HALF = 256
VMEM_LIMIT = 56 * 1024 * 1024

F32 = jnp.float32
BF16 = jnp.bfloat16


def _dot(a, b):
    return jnp.dot(a, b, preferred_element_type=F32)


def _rms(v, g):
    ms = jnp.mean(v * v, axis=-1, keepdims=True)
    return v * lax.rsqrt(ms + RMS_EPS) * g


def _silu(v):
    return v * jax.nn.sigmoid(v)


def _cols(part, c):
    start = part * D_INNER + c * CW
    return slice(start, start + CW)


def _fill_xn_ext(xm_ref, xp_ref, xq_ref, g_ref, xn_s):
    i = pl.program_id(1)
    g = g_ref[...]
    prev = jnp.where(i > 0, _rms(xp_ref[0], g), 0.0)
    nxt = jnp.where(i < pl.num_programs(1) - 1, _rms(xq_ref[0], g), 0.0)
    xn_s[0:HALO, :] = prev.astype(BF16)
    xn_s[HALO:HALO + TS, :] = _rms(xm_ref[0], g).astype(BF16)
    xn_s[HALO + TS:EXT, :] = nxt.astype(BF16)


def _store_lane_tiles(slab_ref, first, val):
    for lt in range(val.shape[-1] // LANES):
        slab_ref[first + lt] = val[:, lt * LANES:(lt + 1) * LANES]


def _conf_filter_kernel(trig_ref, w_ref, o_ref):
    o_ref[...] = jnp.dot(trig_ref[...], w_ref[...], preferred_element_type=F32,
                         precision=lax.Precision.HIGHEST)


def _conf_kernel(xm_ref, xp_ref, xq_ref, g_ref, w_in_ref, fwd_ref, inv_ref, gtab_ref, dw_b_ref,
                 ln_g_ref, ln_b_ref, w_out_ref, o_ref, xn_s, hc_s):
    _fill_xn_ext(xm_ref, xp_ref, xq_ref, g_ref, xn_s)
    hw = EXT // 2

    for c in range(NC):
        cs = _cols(0, c)
        xe = xn_s[...]
        a = _dot(xe, w_in_ref[:, _cols(0, c)])
        gate = _dot(xe, w_in_ref[:, _cols(1, c)])
        h = (a * jax.nn.sigmoid(gate)).astype(BF16)
        spec = _dot(fwd_ref[...], h)
        top, bot = spec[0:hw], spec[hw:EXT]
        g_re_top = gtab_ref[0:hw, cs]
        g_re_bot = gtab_ref[hw:EXT, cs]
        g_im = gtab_ref[EXT:EXT + hw, cs]
        y = jnp.concatenate([top * g_re_top - bot * g_im, top * g_im + bot * g_re_bot], axis=0)
        hc_s[c] = _dot(inv_ref[...], y.astype(BF16)) + dw_b_ref[:, cs]

    tot = jnp.sum(hc_s[0], axis=-1, keepdims=True)
    for c in range(1, NC):
        tot = tot + jnp.sum(hc_s[c], axis=-1, keepdims=True)
    mu = tot * (1.0 / D_INNER)
    sq = jnp.sum(jnp.square(hc_s[0] - mu), axis=-1, keepdims=True)
    for c in range(1, NC):
        sq = sq + jnp.sum(jnp.square(hc_s[c] - mu), axis=-1, keepdims=True)
    rstd = lax.rsqrt(sq * (1.0 / D_INNER) + LN_EPS)

    o_ref[0] = xm_ref[0]
    for c in range(NC):
        cs = _cols(0, c)
        hn = (hc_s[c] - mu) * rstd * ln_g_ref[:, cs] + ln_b_ref[:, cs]
        z = _dot(xn_s[HALO:HALO + TS, :], w_in_ref[:, _cols(2, c)])
        v = (_silu(hn) * _silu(z)).astype(BF16)
        o_ref[0] += _dot(v, w_out_ref[cs, :])


def _pool_fold_kernel(w_in_ref, w_grp_ref, o_ref):
    o_ref[...] = _dot(w_in_ref[...], w_grp_ref[0].astype(BF16)).astype(BF16)


def _pool_kernel(xm_ref, xp_ref, xq_ref, g_ref, w_fold_ref, w_gate_ref, scale_ref, w_out_ref, o_ref,
                 xn_s, u_s):
    _fill_xn_ext(xm_ref, xp_ref, xq_ref, g_ref, xn_s)
    t = pl.program_id(1) * TS + lax.broadcasted_iota(jnp.int32, (TS, 1), 0)
    o_ref[0] = xm_ref[0]

    for c, w in enumerate(POOL_WINDOWS):
        cs = _cols(0, c)
        left = w // 2
        right = w - 1 - left
        _store_lane_tiles(u_s, c * LT, _dot(xn_s[...], w_fold_ref[:, cs]))
        cnt = jnp.minimum(t + right + 1, SEQ) - jnp.maximum(t - left, 0)
        inv_cnt = 1.0 / cnt.astype(F32)
        parts = []
        for lt in range(LT):
            slab = c * LT + lt
            wsum = u_s[slab, HALO - left:HALO - left + TS, :]
            for k in range(1, w):
                wsum = wsum + u_s[slab, HALO - left + k:HALO - left + k + TS, :]
            parts.append(wsum * inv_cnt - u_s[slab, HALO:HALO + TS, :])
        y = jnp.concatenate(parts, axis=-1) * scale_ref[:, cs]
        z = _dot(xn_s[HALO:HALO + TS, :], w_gate_ref[:, cs])
        v = (y * _silu(z)).astype(BF16)
        o_ref[0] += _dot(v, w_out_ref[cs, :])


def _sconv_kernel(xm_ref, xp_ref, xq_ref, g_ref, w_in_ref, conv_w_ref, w_out_ref, fg_ref, o_ref,
                  xn_s, ch_s):
    _fill_xn_ext(xm_ref, xp_ref, xq_ref, g_ref, xn_s)
    o_ref[0] = xm_ref[0]
    base = HALO - SHORT_CONV_WIDTH // 2

    for c in range(NC):
        cs = _cols(0, c)
        xe = xn_s[...]
        cg = _dot(xe, w_in_ref[:, _cols(1, c)])
        hh = _dot(xe, w_in_ref[:, _cols(2, c)])
        _store_lane_tiles(ch_s, c * LT, cg * hh)
        parts = []
        for lt in range(LT):
            slab = c * LT + lt
            ls = slice(c * CW + lt * LANES, c * CW + (lt + 1) * LANES)
            conv = ch_s[slab, base:base + TS, :] * conv_w_ref[0:1, ls]
            for k in range(1, SHORT_CONV_WIDTH):
                conv = conv + ch_s[slab, base + k:base + k + TS, :] * conv_w_ref[k:k + 1, ls]
            parts.append(conv)
        conv = jnp.concatenate(parts, axis=-1)
        xm = xn_s[HALO:HALO + TS, :]
        bg = _dot(xm, w_in_ref[:, cs])
        z = _dot(xm, w_in_ref[:, _cols(3, c)])
        v = (bg * conv * _silu(z)).astype(BF16)
        o_ref[0] += _dot(v, w_out_ref[cs, :])
    o_ref[0] = _rms(o_ref[0], fg_ref[...])


def _fnet_mix_weights_kernel(cc_ref, sc_ref, w_ref, o_ref):
    w = w_ref[0].astype(BF16)
    gd = FNET_GROUP_DIM
    o_ref[0, :, 0:gd] = _dot(cc_ref[...], w).astype(BF16)
    o_ref[0, :, gd:2 * gd] = _dot(sc_ref[...], w).astype(BF16)


def _fnet_in_kernel(x_ref, g_ref, w_in_ref, ab_ref, pq_ref, xn_s, pq_s):
    xn_s[...] = _rms(x_ref[0], g_ref[...]).astype(BF16)
    n1_rows = TS // FFT_N2

    for c in range(NC):
        u = _dot(xn_s[...], w_in_ref[:, _cols(0, c)]).astype(BF16)
        _store_lane_tiles(pq_s, c * 2 * LT, _dot(u, ab_ref[c]))
        for n2 in range(FFT_N2):
            for part in range(2):
                for lt in range(LT):
                    piece = pq_s[(c * 2 + part) * LT + lt, pl.ds(n2, n1_rows, stride=FFT_N2), :]
                    pq_ref[0, c, n2, part, :, lt * LANES:(lt + 1) * LANES] = piece.astype(BF16)


def _fnet_seq_dft_kernel(m_ref, twc_ref, tws_ref, pq_ref, b_ref, f_ref, a_s):
    rt = 0.5 ** 0.5

    k1_half = FFT_N1 // 2

    def mxu_dot(half, rh, n2):
        hl = slice(half * FFT_HALF, (half + 1) * FFT_HALF)
        re = slice(rh * k1_half, (rh + 1) * k1_half)
        im = slice(FFT_N1 + rh * k1_half, FFT_N1 + (rh + 1) * k1_half)
        lhs = jnp.concatenate([m_ref[re, :], m_ref[im, :]], axis=0)
        res = _dot(lhs, pq_ref[0, 0, n2, :, :, hl].reshape(2 * FFT_N1, FFT_HALF))
        a_s[half, n2, re, :] = res[0:k1_half]
        a_s[half, n2, im, :] = res[k1_half:2 * k1_half]

    def row_block(half, rb):
        r0 = rb * FFT_ROWS
        re_rows = pl.ds(r0, FFT_ROWS)
        im_rows = pl.ds(FFT_N1 + r0, FFT_ROWS)
        for lt in range(FFT_HALF // LANES):
            ls = slice(lt * LANES, (lt + 1) * LANES)
            a = [a_s[half, 0, re_rows, ls]]
            b = [None]
            for n2 in range(1, FFT_N2):
                ar = a_s[half, n2, re_rows, ls]
                ai = a_s[half, n2, im_rows, ls]
                tc = twc_ref[n2, re_rows, :]
                ts = tws_ref[n2, re_rows, :]
                a.append(ar * tc + ai * ts)
                b.append(None if n2 == FFT_N2 // 2 else ai * tc - ar * ts)
            s04, d04 = a[0] + a[4], a[0] - a[4]
            p1, p2, p3 = a[1] + a[7], a[2] + a[6], a[3] + a[5]
            p13, t = p1 + p3, rt * (p1 - p3)
            e = s04 + p2
            q1, q2, q3 = b[1] - b[7], b[2] - b[6], b[3] - b[5]
            u = rt * (q1 + q3)
            ca = [e + p13, d04 + t, s04 - p2, d04 - t, e - p13]
            sb = [None, q2 + u, q1 - q3, u - q2]
            out = [ca[0], ca[1] + sb[1], ca[2] + sb[2], ca[3] + sb[3], ca[4],
                   ca[3] - sb[3], ca[2] - sb[2], ca[1] - sb[1]]
            os = slice(half * FFT_HALF + lt * LANES, half * FFT_HALF + (lt + 1) * LANES)
            bias = b_ref[0, :, os]
            for k2 in range(FFT_N2):
                rows = pl.ds(k2 * FFT_N1 + r0, FFT_ROWS)
                f_ref[0, 0, rows, os] = (out[k2] + bias).astype(BF16)

    stages = [(half, rh) for half in range(CW // FFT_HALF) for rh in range(2)]
    blocks_per_stage = k1_half // FFT_ROWS
    assert blocks_per_stage == FFT_N2
    for s in range(len(stages) + 1):
        for n2 in range(FFT_N2):
            if s < len(stages):
                mxu_dot(*stages[s], n2)
            if s > 0:
                p_half, p_rh = stages[s - 1]
                row_block(p_half, p_rh * blocks_per_stage + n2)


def _fnet_out_kernel(x_ref, g_ref, w_in_ref, f_ref, w_out_ref, o_ref, xn_s):
    xn_s[...] = _rms(x_ref[0], g_ref[...]).astype(BF16)
    o_ref[0] = x_ref[0]
    for c in range(NC):
        z = _dot(xn_s[...], w_in_ref[:, _cols(1, c)])
        v = (f_ref[0, c].astype(F32) * _silu(z)).astype(BF16)
        o_ref[0] += _dot(v, w_out_ref[_cols(0, c), :])


def _resident(shape):
    nd = len(shape)
    return pl.BlockSpec(shape, lambda *_: (0,) * nd, pipeline_mode=pl.Buffered(1))


def _params(sem):
    return pltpu.CompilerParams(dimension_semantics=sem, vmem_limit_bytes=VMEM_LIMIT)


def _row_tile(width):
    return pl.BlockSpec((1, TS, width), lambda b, i: (b, i, 0))


def _tiled_layer(body, name, x, consts, scratch):
    per_tile = TS // HALO
    prev = pl.BlockSpec((1, HALO, D_MODEL), lambda b, i: (b, jnp.maximum(i * per_tile - 1, 0), 0))
    nxt = pl.BlockSpec((1, HALO, D_MODEL),
                       lambda b, i: (b, jnp.minimum((i + 1) * per_tile, SEQ // HALO - 1), 0))
    return pl.pallas_call(
        body,
        grid=(BATCH, SEQ // TS),
        in_specs=[_row_tile(D_MODEL), prev, nxt] + [_resident(c.shape) for c in consts],
        out_specs=_row_tile(D_MODEL),
        out_shape=jax.ShapeDtypeStruct((BATCH, SEQ, D_MODEL), F32),
        scratch_shapes=[pltpu.VMEM((EXT, D_MODEL), BF16)] + scratch,
        compiler_params=_params(("arbitrary", "arbitrary")),
        name=name,
    )(x, x, x, *consts)


def _dft_tables(n):
    k = np.arange(n, dtype=np.int64)
    ang = ((k[:, None] * k[None, :]) % n) * (2.0 * np.pi / n)
    return np.cos(ang), np.sin(ang)


def _seq_fft_tables():
    c, s = _dft_tables(FFT_N1)
    m = np.block([[c, -s], [-s, -c]])
    n2 = np.arange(FFT_N2, dtype=np.int64)[:, None]
    k1 = np.arange(FFT_N1, dtype=np.int64)[None, :]
    ang = ((n2 * k1) % SEQ) * (2.0 * np.pi / SEQ)
    bcast = lambda t: np.broadcast_to(t[:, :, None], (FFT_N2, FFT_N1, LANES))
    return (jnp.asarray(m, dtype=BF16), jnp.asarray(bcast(np.cos(ang)), dtype=F32),
            jnp.asarray(bcast(np.sin(ang)), dtype=F32))


def _conv_dft_tables():
    n, hw = EXT, EXT // 2
    t = np.arange(n, dtype=np.int64)
    f = np.arange(hw, dtype=np.int64)
    ang = ((f[:, None] * t[None, :]) % n) * (2.0 * np.pi / n)
    fwd = np.concatenate([np.cos(ang), -np.sin(ang)], axis=0)
    fwd[hw] = np.cos(np.pi * t)
    te = np.arange(HALO, HALO + TS, dtype=np.int64)
    angi = ((te[:, None] * f[None, :]) % n) * (2.0 * np.pi / n)
    inv = np.concatenate([np.cos(angi), -np.sin(angi)], axis=1) * (2.0 / n)
    inv[:, 0] = 1.0 / n
    inv[:, hw] = np.cos(np.pi * te) / n
    lag = CONF_KERNEL // 2 - np.arange(CONF_KERNEL, dtype=np.int64)
    angf = ((f[:, None] * lag[None, :]) % n) * (2.0 * np.pi / n)
    re_bot = np.cos(angf)
    re_bot[0] = np.cos(np.pi * lag)
    trig = np.zeros((n + hw, LANES))
    trig[:, :CONF_KERNEL] = np.concatenate([np.cos(angf), re_bot, -np.sin(angf)], axis=0)
    return jnp.asarray(fwd, dtype=BF16), jnp.asarray(inv, dtype=BF16), jnp.asarray(trig, dtype=F32)


def _fnet_layer(x, g, w_in, w_mix, b_mix, w_out):
    gd = FNET_GROUP_DIM
    ortho = 1.0 / (SEQ * gd) ** 0.5
    cc, sc = _dft_tables(gd)
    ab = pl.pallas_call(
        _fnet_mix_weights_kernel,
        grid=(FNET_GROUPS,),
        in_specs=[_resident((gd, gd)), _resident((gd, gd)),
                  pl.BlockSpec((1, gd, gd), lambda c: (c, 0, 0))],
        out_specs=pl.BlockSpec((1, gd, 2 * gd), lambda c: (c, 0, 0)),
        out_shape=jax.ShapeDtypeStruct((FNET_GROUPS, gd, 2 * gd), BF16),
        compiler_params=_params(("arbitrary",)),
        name="fnet_mix_weights",
    )(jnp.asarray(cc * ortho, dtype=BF16), jnp.asarray(sc * ortho, dtype=BF16), w_mix)

    pq = pl.pallas_call(
        _fnet_in_kernel,
        grid=(BATCH, SEQ // TS),
        in_specs=[_row_tile(D_MODEL), _resident(g.shape), _resident(w_in.shape), _resident(ab.shape)],
        out_specs=pl.BlockSpec((1, NC, FFT_N2, 2, TS // FFT_N2, CW), lambda b, i: (b, 0, 0, 0, i, 0)),
        out_shape=jax.ShapeDtypeStruct((BATCH, NC, FFT_N2, 2, FFT_N1, CW), BF16),
        scratch_shapes=[pltpu.VMEM((TS, D_MODEL), BF16), pltpu.VMEM((NC * 2 * LT, TS, LANES), F32)],
        compiler_params=_params(("arbitrary", "arbitrary")),
        name="fnet_in",
    )(x, g, w_in, ab)

    m, twc, tws = _seq_fft_tables()
    f = pl.pallas_call(
        _fnet_seq_dft_kernel,
        grid=(BATCH, NC),
        in_specs=[_resident(m.shape), _resident(twc.shape), _resident(tws.shape),
                  pl.BlockSpec((1, 1, FFT_N2, 2, FFT_N1, CW), lambda b, c: (b, c, 0, 0, 0, 0)),
                  pl.BlockSpec((1, 1, CW), lambda b, c: (c, 0, 0))],
        out_specs=pl.BlockSpec((1, 1, SEQ, CW), lambda b, c: (b, c, 0, 0)),
        out_shape=jax.ShapeDtypeStruct((BATCH, NC, SEQ, CW), BF16),
        scratch_shapes=[pltpu.VMEM((CW // FFT_HALF, FFT_N2, 2 * FFT_N1, FFT_HALF), F32)],
        compiler_params=_params(("arbitrary", "arbitrary")),
        name="fnet_seq_dft",
    )(m, twc, tws, pq, b_mix.reshape(NC, 1, CW))

    return pl.pallas_call(
        _fnet_out_kernel,
        grid=(BATCH, SEQ // TS),
        in_specs=[_row_tile(D_MODEL), _resident(g.shape), _resident(w_in.shape),
                  pl.BlockSpec((1, NC, TS, CW), lambda b, i: (b, 0, i, 0)), _resident(w_out.shape)],
        out_specs=_row_tile(D_MODEL),
        out_shape=jax.ShapeDtypeStruct((BATCH, SEQ, D_MODEL), F32),
        scratch_shapes=[pltpu.VMEM((TS, D_MODEL), BF16)],
        compiler_params=_params(("arbitrary", "arbitrary")),
        name="fnet_out",
    )(x, g, w_in, f, w_out)


def kernel(x, norm_g, w_out, final_g, fnet_w_in, fnet_w_mix, fnet_b_mix, conf_w_in, conf_dw_w, conf_dw_b, conf_ln_g, conf_ln_b, pool_w_in, pool_w_grp, pool_scale, sc_w_in, sc_conv_w):
    assert x.shape == (BATCH, SEQ, D_MODEL) and x.dtype == F32
    assert fnet_w_in.shape[0] == conf_w_in.shape[0] == pool_w_in.shape[0] == sc_w_in.shape[0] == 1
    g = norm_g.reshape(4, 1, D_MODEL)
    w_out_b = w_out.astype(BF16)
    slabs = pltpu.VMEM((NC * LT, EXT, LANES), F32)

    x = _fnet_layer(x, g[0], fnet_w_in[0].astype(BF16), fnet_w_mix[0], fnet_b_mix[0], w_out_b[0])

    fwd, inv, trig = _conv_dft_tables()
    dw_w = jnp.pad(conf_dw_w[0], ((0, LANES - CONF_KERNEL), (0, 0)))
    gtab = pl.pallas_call(
        _conf_filter_kernel,
        grid=(NC,),
        in_specs=[_resident(trig.shape), pl.BlockSpec((LANES, CW), lambda c: (0, c))],
        out_specs=pl.BlockSpec((EXT + EXT // 2, CW), lambda c: (0, c)),
        out_shape=jax.ShapeDtypeStruct((EXT + EXT // 2, D_INNER), F32),
        compiler_params=_params(("arbitrary",)),
        name="conformer_filter_spectrum",
    )(trig, dw_w)
    x = _tiled_layer(
        _conf_kernel, "conformer_layer", x,
        [g[1], conf_w_in[0].astype(BF16), fwd, inv, gtab, conf_dw_b, conf_ln_g, conf_ln_b, w_out_b[1]],
        [pltpu.VMEM((NC, TS, CW), F32)])

    pool_w = pool_w_in[0].astype(BF16)
    w_fold = pl.pallas_call(
        _pool_fold_kernel,
        grid=(NC,),
        in_specs=[pl.BlockSpec((D_MODEL, CW), lambda c: (0, c)),
                  pl.BlockSpec((1, CW, CW), lambda c: (c, 0, 0))],
        out_specs=pl.BlockSpec((D_MODEL, CW), lambda c: (0, c)),
        out_shape=jax.ShapeDtypeStruct((D_MODEL, D_INNER), BF16),
        compiler_params=_params(("arbitrary",)),
        name="pool_fold_weights",
    )(pool_w, pool_w_grp[0])
    x = _tiled_layer(
        _pool_kernel, "pool_layer", x,
        [g[2], w_fold, pool_w[:, D_INNER:], pool_scale, w_out_b[2]],
        [slabs])

    x = _tiled_layer(
        _sconv_kernel, "short_conv_layer", x,
        [g[3], sc_w_in[0].astype(BF16), sc_conv_w[0], w_out_b[3], final_g.reshape(1, D_MODEL)],
        [slabs])
    return x
```

```python
import jax
import jax.numpy as jnp
import numpy as np
from jax import lax
from jax.experimental import pallas as pl
from jax.experimental.pallas import tpu as pltpu

D_MODEL = 1024
D_INNER = 2048
BATCH = 8
SEQ = 2048
FNET_GROUPS = 4
FNET_GROUP_DIM = D_INNER // FNET_GROUPS
CONF_KERNEL = 31
POOL_WINDOWS = (2, 4, 8, 16)
SHORT_CONV_WIDTH = 3
RMS_EPS = 1e-6
LN_EPS = 1e-5

TS = 512
TS_WIDE = 1024
HALO = 16
EXT = TS + 2 * HALO
CW = 512
NC = D_INNER // CW
LANES = 128
LT = CW // LANES
FFT_N1 = 256
FFT_N2 = SEQ // FFT_N1
FFT_ROWS = 16
FFT_HALF = 256
VMEM_LIMIT = 56 * 1024 * 1024

F32 = jnp.float32
BF16 = jnp.bfloat16


def _dot(a, b):
    return jnp.dot(a, b, preferred_element_type=F32)


def _rms(v, g):
    ms = jnp.mean(v * v, axis=-1, keepdims=True)
    return v * lax.rsqrt(ms + RMS_EPS) * g


def _silu(v):
    return v * jax.nn.sigmoid(v)


def _cols(part, c):
    start = part * D_INNER + c * CW
    return slice(start, start + CW)


def _fill_xn_ext(xm_ref, xp_ref, xq_ref, g_ref, xn_s):
    i = pl.program_id(1)
    ts = xm_ref.shape[1]
    g = g_ref[...]
    prev = jnp.where(i > 0, _rms(xp_ref[0], g), 0.0)
    nxt = jnp.where(i < pl.num_programs(1) - 1, _rms(xq_ref[0], g), 0.0)
    xn_s[0:HALO, :] = prev.astype(BF16)
    xn_s[HALO:HALO + ts, :] = _rms(xm_ref[0], g).astype(BF16)
    xn_s[HALO + ts:ts + 2 * HALO, :] = nxt.astype(BF16)


def _store_lane_tiles(slab_ref, first, val):
    for lt in range(val.shape[-1] // LANES):
        slab_ref[first + lt] = val[:, lt * LANES:(lt + 1) * LANES]


def _conf_filter_kernel(trig_ref, w_ref, o_ref):
    o_ref[...] = jnp.dot(trig_ref[...], w_ref[...], preferred_element_type=F32,
                         precision=lax.Precision.HIGHEST)


def _conf_kernel(xm_ref, xp_ref, xq_ref, g_ref, w_in_ref, fwd_ref, inv_ref, gtab_ref, dw_b_ref,
                 ln_g_ref, ln_b_ref, w_out_ref, o_ref, xn_s, hc_s):
    _fill_xn_ext(xm_ref, xp_ref, xq_ref, g_ref, xn_s)
    hw = EXT // 2

    for c in range(NC):
        cs = _cols(0, c)
        xe = xn_s[...]
        a = _dot(xe, w_in_ref[:, _cols(0, c)])
        gate = _dot(xe, w_in_ref[:, _cols(1, c)])
        h = (a * jax.nn.sigmoid(gate)).astype(BF16)
        spec = _dot(fwd_ref[...], h)
        top, bot = spec[0:hw], spec[hw:EXT]
        g_re_top = gtab_ref[0:hw, cs]
        g_re_bot = gtab_ref[hw:EXT, cs]
        g_im = gtab_ref[EXT:EXT + hw, cs]
        y = jnp.concatenate([top * g_re_top - bot * g_im, top * g_im + bot * g_re_bot], axis=0)
        hc_s[c] = _dot(inv_ref[...], y.astype(BF16)) + dw_b_ref[:, cs]

    tot = jnp.sum(hc_s[0], axis=-1, keepdims=True)
    for c in range(1, NC):
        tot = tot + jnp.sum(hc_s[c], axis=-1, keepdims=True)
    mu = tot * (1.0 / D_INNER)
    sq = jnp.sum(jnp.square(hc_s[0] - mu), axis=-1, keepdims=True)
    for c in range(1, NC):
        sq = sq + jnp.sum(jnp.square(hc_s[c] - mu), axis=-1, keepdims=True)
    rstd = lax.rsqrt(sq * (1.0 / D_INNER) + LN_EPS)

    o_ref[0] = xm_ref[0]
    for c in range(NC):
        cs = _cols(0, c)
        hn = (hc_s[c] - mu) * rstd * ln_g_ref[:, cs] + ln_b_ref[:, cs]
        z = _dot(xn_s[HALO:HALO + TS, :], w_in_ref[:, _cols(2, c)])
        v = (_silu(hn) * _silu(z)).astype(BF16)
        o_ref[0] += _dot(v, w_out_ref[cs, :])


def _pool_fold_kernel(w_in_ref, w_grp_ref, o_ref):
    o_ref[...] = _dot(w_in_ref[...], w_grp_ref[0].astype(BF16)).astype(BF16)


def _pool_kernel(xm_ref, xp_ref, xq_ref, g_ref, w_fold_ref, w_gate_ref, scale_ref, w_out_ref, o_ref,
                 xn_s, u_s):
    _fill_xn_ext(xm_ref, xp_ref, xq_ref, g_ref, xn_s)
    ts = xm_ref.shape[1]
    t = pl.program_id(1) * ts + lax.broadcasted_iota(jnp.int32, (ts, 1), 0)
    o_ref[0] = xm_ref[0]

    for c, w in enumerate(POOL_WINDOWS):
        cs = _cols(0, c)
        left = w // 2
        right = w - 1 - left
        _store_lane_tiles(u_s, c * LT, _dot(xn_s[...], w_fold_ref[:, cs]))
        cnt = jnp.minimum(t + right + 1, SEQ) - jnp.maximum(t - left, 0)
        inv_cnt = 1.0 / cnt.astype(F32)
        parts = []
        for lt in range(LT):
            slab = c * LT + lt
            wsum = u_s[slab, HALO - left:HALO - left + ts, :]
            for k in range(1, w):
                wsum = wsum + u_s[slab, HALO - left + k:HALO - left + k + ts, :]
            parts.append(wsum * inv_cnt - u_s[slab, HALO:HALO + ts, :])
        y = jnp.concatenate(parts, axis=-1) * scale_ref[:, cs]
        z = _dot(xn_s[HALO:HALO + ts, :], w_gate_ref[:, cs])
        v = (y * _silu(z)).astype(BF16)
        o_ref[0] += _dot(v, w_out_ref[cs, :])


def _sconv_kernel(xm_ref, xp_ref, xq_ref, g_ref, w_in_ref, conv_w_ref, w_out_ref, fg_ref, o_ref,
                  xn_s, ch_s):
    _fill_xn_ext(xm_ref, xp_ref, xq_ref, g_ref, xn_s)
    ts = xm_ref.shape[1]
    o_ref[0] = xm_ref[0]
    base = HALO - SHORT_CONV_WIDTH // 2

    for c in range(NC):
        cs = _cols(0, c)
        xe = xn_s[...]
        cg = _dot(xe, w_in_ref[:, _cols(1, c)])
        hh = _dot(xe, w_in_ref[:, _cols(2, c)])
        _store_lane_tiles(ch_s, c * LT, cg * hh)
        parts = []
        for lt in range(LT):
            slab = c * LT + lt
            ls = slice(c * CW + lt * LANES, c * CW + (lt + 1) * LANES)
            conv = ch_s[slab, base:base + ts, :] * conv_w_ref[0:1, ls]
            for k in range(1, SHORT_CONV_WIDTH):
                conv = conv + ch_s[slab, base + k:base + k + ts, :] * conv_w_ref[k:k + 1, ls]
            parts.append(conv)
        conv = jnp.concatenate(parts, axis=-1)
        xm = xn_s[HALO:HALO + ts, :]
        bg = _dot(xm, w_in_ref[:, cs])
        z = _dot(xm, w_in_ref[:, _cols(3, c)])
        v = (bg * conv * _silu(z)).astype(BF16)
        o_ref[0] += _dot(v, w_out_ref[cs, :])
    o_ref[0] = _rms(o_ref[0], fg_ref[...])


def _fnet_mix_weights_kernel(cc_ref, sc_ref, w_ref, o_ref):
    w = w_ref[0].astype(BF16)
    gd = FNET_GROUP_DIM
    o_ref[0, :, 0:gd] = _dot(cc_ref[...], w).astype(BF16)
    o_ref[0, :, gd:2 * gd] = _dot(sc_ref[...], w).astype(BF16)


def _fnet_in_kernel(x_ref, g_ref, w_in_ref, ab_ref, pq_ref, xn_s, pq_s):
    xn_s[...] = _rms(x_ref[0], g_ref[...]).astype(BF16)
    n1_rows = x_ref.shape[1] // FFT_N2
    slab_sets = pq_s.shape[0] // (2 * LT)

    for c in range(NC):
        first = (c % slab_sets) * 2 * LT
        u = _dot(xn_s[...], w_in_ref[:, _cols(0, c)]).astype(BF16)
        _store_lane_tiles(pq_s, first, _dot(u, ab_ref[c]))
        for n2 in range(FFT_N2):
            for part in range(2):
                for lt in range(LT):
                    piece = pq_s[first + part * LT + lt, pl.ds(n2, n1_rows, stride=FFT_N2), :]
                    pq_ref[0, c, n2, part, :, lt * LANES:(lt + 1) * LANES] = piece.astype(BF16)


def _fnet_seq_dft_kernel(m_ref, twc_ref, tws_ref, pq_ref, b_ref, f_ref, a_s):
    rt = 0.5 ** 0.5

    k1_half = FFT_N1 // 2

    def mxu_dot(half, rh, n2):
        hl = slice(half * FFT_HALF, (half + 1) * FFT_HALF)
        re = slice(rh * k1_half, (rh + 1) * k1_half)
        im = slice(FFT_N1 + rh * k1_half, FFT_N1 + (rh + 1) * k1_half)
        lhs = jnp.concatenate([m_ref[re, :], m_ref[im, :]], axis=0)
        res = _dot(lhs, pq_ref[0, 0, n2, :, :, hl].reshape(2 * FFT_N1, FFT_HALF))
        a_s[half, n2, re, :] = res[0:k1_half]
        a_s[half, n2, im, :] = res[k1_half:2 * k1_half]

    def row_block(half, rb):
        r0 = rb * FFT_ROWS
        re_rows = pl.ds(r0, FFT_ROWS)
        im_rows = pl.ds(FFT_N1 + r0, FFT_ROWS)
        for lt in range(FFT_HALF // LANES):
            ls = slice(lt * LANES, (lt + 1) * LANES)
            a = [a_s[half, 0, re_rows, ls]]
            b = [None]
            for n2 in range(1, FFT_N2):
                ar = a_s[half, n2, re_rows, ls]
                ai = a_s[half, n2, im_rows, ls]
                tc = twc_ref[n2, re_rows, :]
                ts = tws_ref[n2, re_rows, :]
                a.append(ar * tc + ai * ts)
                b.append(None if n2 == FFT_N2 // 2 else ai * tc - ar * ts)
            s04, d04 = a[0] + a[4], a[0] - a[4]
            p1, p2, p3 = a[1] + a[7], a[2] + a[6], a[3] + a[5]
            p13, t = p1 + p3, rt * (p1 - p3)
            e = s04 + p2
            q1, q2, q3 = b[1] - b[7], b[2] - b[6], b[3] - b[5]
            u = rt * (q1 + q3)
            ca = [e + p13, d04 + t, s04 - p2, d04 - t, e - p13]
            sb = [None, q2 + u, q1 - q3, u - q2]
            out = [ca[0], ca[1] + sb[1], ca[2] + sb[2], ca[3] + sb[3], ca[4],
                   ca[3] - sb[3], ca[2] - sb[2], ca[1] - sb[1]]
            os = slice(half * FFT_HALF + lt * LANES, half * FFT_HALF + (lt + 1) * LANES)
            bias = b_ref[0, :, os]
            for k2 in range(FFT_N2):
                rows = pl.ds(k2 * FFT_N1 + r0, FFT_ROWS)
                f_ref[0, 0, rows, os] = (out[k2] + bias).astype(BF16)

    stages = [(half, rh) for half in range(CW // FFT_HALF) for rh in range(2)]
    blocks_per_stage = k1_half // FFT_ROWS
    assert blocks_per_stage == FFT_N2
    for s in range(len(stages) + 1):
        for n2 in range(FFT_N2):
            if s < len(stages):
                mxu_dot(*stages[s], n2)
            if s > 0:
                p_half, p_rh = stages[s - 1]
                row_block(p_half, p_rh * blocks_per_stage + n2)


def _fnet_out_kernel(x_ref, g_ref, w_gate_ref, f_ref, w_out_ref, o_ref, xn_s):
    xn_s[...] = _rms(x_ref[0], g_ref[...]).astype(BF16)
    o_ref[0] = x_ref[0]
    for c in range(NC):
        z = _dot(xn_s[...], w_gate_ref[:, _cols(0, c)])
        v = (f_ref[0, c].astype(F32) * _silu(z)).astype(BF16)
        o_ref[0] += _dot(v, w_out_ref[_cols(0, c), :])


def _resident(shape):
    nd = len(shape)
    return pl.BlockSpec(shape, lambda *_: (0,) * nd, pipeline_mode=pl.Buffered(1))


def _params(sem):
    return pltpu.CompilerParams(dimension_semantics=sem, vmem_limit_bytes=VMEM_LIMIT)


def _row_tile(ts):
    return pl.BlockSpec((1, ts, D_MODEL), lambda b, i: (b, i, 0))


def _tiled_layer(body, name, x, consts, scratch, ts):
    per_tile = ts // HALO
    prev = pl.BlockSpec((1, HALO, D_MODEL), lambda b, i: (b, jnp.maximum(i * per_tile - 1, 0), 0))
    nxt = pl.BlockSpec((1, HALO, D_MODEL),
                       lambda b, i: (b, jnp.minimum((i + 1) * per_tile, SEQ // HALO - 1), 0))
    return pl.pallas_call(
        body,
        grid=(BATCH, SEQ // ts),
        in_specs=[_row_tile(ts), prev, nxt] + [_resident(c.shape) for c in consts],
        out_specs=_row_tile(ts),
        out_shape=jax.ShapeDtypeStruct((BATCH, SEQ, D_MODEL), F32),
        scratch_shapes=[pltpu.VMEM((ts + 2 * HALO, D_MODEL), BF16)] + scratch,
        compiler_params=_params(("arbitrary", "arbitrary")),
        name=name,
    )(x, x, x, *consts)


def _dft_tables(n):
    k = np.arange(n, dtype=np.int64)
    ang = ((k[:, None] * k[None, :]) % n) * (2.0 * np.pi / n)
    return np.cos(ang), np.sin(ang)


def _seq_fft_tables():
    c, s = _dft_tables(FFT_N1)
    m = np.block([[c, -s], [-s, -c]])
    n2 = np.arange(FFT_N2, dtype=np.int64)[:, None]
    k1 = np.arange(FFT_N1, dtype=np.int64)[None, :]
    ang = ((n2 * k1) % SEQ) * (2.0 * np.pi / SEQ)
    bcast = lambda t: np.broadcast_to(t[:, :, None], (FFT_N2, FFT_N1, LANES))
    return (jnp.asarray(m, dtype=BF16), jnp.asarray(bcast(np.cos(ang)), dtype=F32),
            jnp.asarray(bcast(np.sin(ang)), dtype=F32))


def _conv_dft_tables():
    n, hw = EXT, EXT // 2
    t = np.arange(n, dtype=np.int64)
    f = np.arange(hw, dtype=np.int64)
    ang = ((f[:, None] * t[None, :]) % n) * (2.0 * np.pi / n)
    fwd = np.concatenate([np.cos(ang), -np.sin(ang)], axis=0)
    fwd[hw] = np.cos(np.pi * t)
    te = np.arange(HALO, HALO + TS, dtype=np.int64)
    angi = ((te[:, None] * f[None, :]) % n) * (2.0 * np.pi / n)
    inv = np.concatenate([np.cos(angi), -np.sin(angi)], axis=1) * (2.0 / n)
    inv[:, 0] = 1.0 / n
    inv[:, hw] = np.cos(np.pi * te) / n
    lag = CONF_KERNEL // 2 - np.arange(CONF_KERNEL, dtype=np.int64)
    angf = ((f[:, None] * lag[None, :]) % n) * (2.0 * np.pi / n)
    re_bot = np.cos(angf)
    re_bot[0] = np.cos(np.pi * lag)
    trig = np.zeros((n + hw, LANES))
    trig[:, :CONF_KERNEL] = np.concatenate([np.cos(angf), re_bot, -np.sin(angf)], axis=0)
    return jnp.asarray(fwd, dtype=BF16), jnp.asarray(inv, dtype=BF16), jnp.asarray(trig, dtype=F32)


def _fnet_layer(x, g, w_branch, w_gate, w_mix, b_mix, w_out):
    gd = FNET_GROUP_DIM
    ortho = 1.0 / (SEQ * gd) ** 0.5
    cc, sc = _dft_tables(gd)
    ab = pl.pallas_call(
        _fnet_mix_weights_kernel,
        grid=(FNET_GROUPS,),
        in_specs=[_resident((gd, gd)), _resident((gd, gd)),
                  pl.BlockSpec((1, gd, gd), lambda c: (c, 0, 0))],
        out_specs=pl.BlockSpec((1, gd, 2 * gd), lambda c: (c, 0, 0)),
        out_shape=jax.ShapeDtypeStruct((FNET_GROUPS, gd, 2 * gd), BF16),
        compiler_params=_params(("arbitrary",)),
        name="fnet_mix_weights",
    )(jnp.asarray(cc * ortho, dtype=BF16), jnp.asarray(sc * ortho, dtype=BF16), w_mix)

    ts = TS_WIDE
    pq = pl.pallas_call(
        _fnet_in_kernel,
        grid=(BATCH, SEQ // ts),
        in_specs=[_row_tile(ts), _resident(g.shape), _resident(w_branch.shape), _resident(ab.shape)],
        out_specs=pl.BlockSpec((1, NC, FFT_N2, 2, ts // FFT_N2, CW), lambda b, i: (b, 0, 0, 0, i, 0)),
        out_shape=jax.ShapeDtypeStruct((BATCH, NC, FFT_N2, 2, FFT_N1, CW), BF16),
        scratch_shapes=[pltpu.VMEM((ts, D_MODEL), BF16), pltpu.VMEM((2 * 2 * LT, ts, LANES), F32)],
        compiler_params=_params(("arbitrary", "arbitrary")),
        name="fnet_in",
    )(x, g, w_branch, ab)

    m, twc, tws = _seq_fft_tables()
    f = pl.pallas_call(
        _fnet_seq_dft_kernel,
        grid=(BATCH, NC),
        in_specs=[_resident(m.shape), _resident(twc.shape), _resident(tws.shape),
                  pl.BlockSpec((1, 1, FFT_N2, 2, FFT_N1, CW), lambda b, c: (b, c, 0, 0, 0, 0)),
                  pl.BlockSpec((1, 1, CW), lambda b, c: (c, 0, 0))],
        out_specs=pl.BlockSpec((1, 1, SEQ, CW), lambda b, c: (b, c, 0, 0)),
        out_shape=jax.ShapeDtypeStruct((BATCH, NC, SEQ, CW), BF16),
        scratch_shapes=[pltpu.VMEM((CW // FFT_HALF, FFT_N2, 2 * FFT_N1, FFT_HALF), F32)],
        compiler_params=_params(("arbitrary", "arbitrary")),
        name="fnet_seq_dft",
    )(m, twc, tws, pq, b_mix.reshape(NC, 1, CW))

    return pl.pallas_call(
        _fnet_out_kernel,
        grid=(BATCH, SEQ // ts),
        in_specs=[_row_tile(ts), _resident(g.shape), _resident(w_gate.shape),
                  pl.BlockSpec((1, NC, ts, CW), lambda b, i: (b, 0, i, 0)), _resident(w_out.shape)],
        out_specs=_row_tile(ts),
        out_shape=jax.ShapeDtypeStruct((BATCH, SEQ, D_MODEL), F32),
        scratch_shapes=[pltpu.VMEM((ts, D_MODEL), BF16)],
        compiler_params=_params(("arbitrary", "arbitrary")),
        name="fnet_out",
    )(x, g, w_gate, f, w_out)


def kernel(x, norm_g, w_out, final_g, fnet_w_in, fnet_w_mix, fnet_b_mix, conf_w_in, conf_dw_w, conf_dw_b, conf_ln_g, conf_ln_b, pool_w_in, pool_w_grp, pool_scale, sc_w_in, sc_conv_w):
    assert x.shape == (BATCH, SEQ, D_MODEL) and x.dtype == F32
    assert fnet_w_in.shape[0] == conf_w_in.shape[0] == pool_w_in.shape[0] == sc_w_in.shape[0] == 1
    g = norm_g.reshape(4, 1, D_MODEL)
    w_out_b = w_out.astype(BF16)
    slabs = pltpu.VMEM((NC * LT, TS_WIDE + 2 * HALO, LANES), F32)

    x = _fnet_layer(x, g[0], fnet_w_in[0, :, :D_INNER].astype(BF16), fnet_w_in[0, :, D_INNER:].astype(BF16),
                    fnet_w_mix[0], fnet_b_mix[0], w_out_b[0])

    fwd, inv, trig = _conv_dft_tables()
    dw_w = jnp.pad(conf_dw_w[0], ((0, LANES - CONF_KERNEL), (0, 0)))
    gtab = pl.pallas_call(
        _conf_filter_kernel,
        grid=(NC,),
        in_specs=[_resident(trig.shape), pl.BlockSpec((LANES, CW), lambda c: (0, c))],
        out_specs=pl.BlockSpec((EXT + EXT // 2, CW), lambda c: (0, c)),
        out_shape=jax.ShapeDtypeStruct((EXT + EXT // 2, D_INNER), F32),
        compiler_params=_params(("arbitrary",)),
        name="conformer_filter_spectrum",
    )(trig, dw_w)
    x = _tiled_layer(
        _conf_kernel, "conformer_layer", x,
        [g[1], conf_w_in[0].astype(BF16), fwd, inv, gtab, conf_dw_b, conf_ln_g, conf_ln_b, w_out_b[1]],
        [pltpu.VMEM((NC, TS, CW), F32)], TS)

    pool_w = pool_w_in[0].astype(BF16)
    w_fold = pl.pallas_call(
        _pool_fold_kernel,
        grid=(NC,),
        in_specs=[pl.BlockSpec((D_MODEL, CW), lambda c: (0, c)),
                  pl.BlockSpec((1, CW, CW), lambda c: (c, 0, 0))],
        out_specs=pl.BlockSpec((D_MODEL, CW), lambda c: (0, c)),
        out_shape=jax.ShapeDtypeStruct((D_MODEL, D_INNER), BF16),
        compiler_params=_params(("arbitrary",)),
        name="pool_fold_weights",
    )(pool_w, pool_w_grp[0])
    x = _tiled_layer(
        _pool_kernel, "pool_layer", x,
        [g[2], w_fold, pool_w[:, D_INNER:], pool_scale, w_out_b[2]],
        [slabs], TS_WIDE)

    x = _tiled_layer(
        _sconv_kernel, "short_conv_layer", x,
        [g[3], sc_w_in[0].astype(BF16), sc_conv_w[0], w_out_b[3], final_g.reshape(1, D_MODEL)],
        [slabs], TS_WIDE)
    return x
```

```python
import jax
import jax.numpy as jnp
import numpy as np
from jax import lax
from jax.experimental import pallas as pl
from jax.experimental.pallas import tpu as pltpu

D_MODEL = 1024
D_INNER = 2048
BATCH = 8
SEQ = 2048
FNET_GROUPS = 4
FNET_GROUP_DIM = D_INNER // FNET_GROUPS
CONF_KERNEL = 31
POOL_WINDOWS = (2, 4, 8, 16)
SHORT_CONV_WIDTH = 3
RMS_EPS = 1e-6
LN_EPS = 1e-5

TS = 512
TS_WIDE = 1024
HALO = 16
EXT = TS + 2 * HALO
CONV_N = 256
CONV_BLOCKS = ((0, 16, 240), (208, 32, 240), (288, 160, 240))
CW = 512
NC = D_INNER // CW
LANES = 128
LT = CW // LANES
FFT_N1 = 256
FFT_N2 = SEQ // FFT_N1
FFT_ROWS = 16
FFT_HALF = 256
VMEM_LIMIT = 56 * 1024 * 1024

F32 = jnp.float32
BF16 = jnp.bfloat16


def _dot(a, b):
    return jnp.dot(a, b, preferred_element_type=F32)


def _rms(v, g):
    ms = jnp.mean(v * v, axis=-1, keepdims=True)
    return v * lax.rsqrt(ms + RMS_EPS) * g


def _silu(v):
    return v * jax.nn.sigmoid(v)


def _cols(part, c):
    start = part * D_INNER + c * CW
    return slice(start, start + CW)


def _fill_xn_ext(xm_ref, xp_ref, xq_ref, g_ref, xn_s):
    i = pl.program_id(1)
    ts = xm_ref.shape[1]
    g = g_ref[...]
    prev = jnp.where(i > 0, _rms(xp_ref[0], g), 0.0)
    nxt = jnp.where(i < pl.num_programs(1) - 1, _rms(xq_ref[0], g), 0.0)
    xn_s[0:HALO, :] = prev.astype(BF16)
    xn_s[HALO:HALO + ts, :] = _rms(xm_ref[0], g).astype(BF16)
    xn_s[HALO + ts:ts + 2 * HALO, :] = nxt.astype(BF16)


def _store_lane_tiles(slab_ref, first, val):
    for lt in range(val.shape[-1] // LANES):
        slab_ref[first + lt] = val[:, lt * LANES:(lt + 1) * LANES]


def _conf_filter_kernel(trig_ref, w_ref, o_ref):
    o_ref[...] = jnp.dot(trig_ref[...], w_ref[...], preferred_element_type=F32,
                         precision=lax.Precision.HIGHEST)


def _conf_kernel(xm_ref, xp_ref, xq_ref, g_ref, w_in_ref, fwd_ref, inv_ref, gtab_ref, dw_b_ref,
                 ln_g_ref, ln_b_ref, w_out_ref, o_ref, xn_s, hc_s):
    _fill_xn_ext(xm_ref, xp_ref, xq_ref, g_ref, xn_s)
    hw = CONV_N // 2

    for c in range(NC):
        cs = _cols(0, c)
        xe = xn_s[...]
        a = _dot(xe, w_in_ref[:, _cols(0, c)])
        gate = _dot(xe, w_in_ref[:, _cols(1, c)])
        h = (a * jax.nn.sigmoid(gate)).astype(BF16)
        nb = len(CONV_BLOCKS)
        tile3 = lambda v: jnp.concatenate([v] * nb, axis=-1)
        g_re_top = tile3(gtab_ref[0:hw, cs])
        g_re_bot = tile3(gtab_ref[hw:CONV_N, cs])
        g_im = tile3(gtab_ref[CONV_N:CONV_N + hw, cs])
        bias = dw_b_ref[:, cs]
        hb = jnp.concatenate([h[off:off + CONV_N] for off, _, _ in CONV_BLOCKS], axis=-1)
        spec = _dot(fwd_ref[...], hb)
        top, bot = spec[0:hw], spec[hw:CONV_N]
        y = jnp.concatenate([top * g_re_top - bot * g_im, top * g_im + bot * g_re_bot], axis=0)
        conv = _dot(inv_ref[...], y.astype(BF16))
        for j, (off, lo, hi) in enumerate(CONV_BLOCKS):
            t0 = off + lo - HALO
            hc_s[c, t0:t0 + hi - lo, :] = conv[lo:hi, j * CW:(j + 1) * CW] + bias

    tot = jnp.sum(hc_s[0], axis=-1, keepdims=True)
    for c in range(1, NC):
        tot = tot + jnp.sum(hc_s[c], axis=-1, keepdims=True)
    mu = tot * (1.0 / D_INNER)
    sq = jnp.sum(jnp.square(hc_s[0] - mu), axis=-1, keepdims=True)
    for c in range(1, NC):
        sq = sq + jnp.sum(jnp.square(hc_s[c] - mu), axis=-1, keepdims=True)
    rstd = lax.rsqrt(sq * (1.0 / D_INNER) + LN_EPS)

    o_ref[0] = xm_ref[0]
    for c in range(NC):
        cs = _cols(0, c)
        hn = (hc_s[c] - mu) * rstd * ln_g_ref[:, cs] + ln_b_ref[:, cs]
        z = _dot(xn_s[HALO:HALO + TS, :], w_in_ref[:, _cols(2, c)])
        v = (_silu(hn) * _silu(z)).astype(BF16)
        o_ref[0] += _dot(v, w_out_ref[cs, :])


def _pool_fold_kernel(w_in_ref, w_grp_ref, o_ref):
    o_ref[...] = _dot(w_in_ref[...], w_grp_ref[0].astype(BF16)).astype(BF16)


def _pool_kernel(xm_ref, xp_ref, xq_ref, g_ref, w_fold_ref, w_gate_ref, scale_ref, w_out_ref, o_ref,
                 xn_s, u_s):
    _fill_xn_ext(xm_ref, xp_ref, xq_ref, g_ref, xn_s)
    ts = xm_ref.shape[1]
    t = pl.program_id(1) * ts + lax.broadcasted_iota(jnp.int32, (ts, 1), 0)
    o_ref[0] = xm_ref[0]

    for c, w in enumerate(POOL_WINDOWS):
        cs = _cols(0, c)
        left = w // 2
        right = w - 1 - left
        _store_lane_tiles(u_s, c * LT, _dot(xn_s[...], w_fold_ref[:, cs]))
        cnt = jnp.minimum(t + right + 1, SEQ) - jnp.maximum(t - left, 0)
        inv_cnt = 1.0 / cnt.astype(F32)
        parts = []
        for lt in range(LT):
            slab = c * LT + lt
            wsum = u_s[slab, HALO - left:HALO - left + ts, :]
            for k in range(1, w):
                wsum = wsum + u_s[slab, HALO - left + k:HALO - left + k + ts, :]
            parts.append(wsum * inv_cnt - u_s[slab, HALO:HALO + ts, :])
        y = jnp.concatenate(parts, axis=-1) * scale_ref[:, cs]
        z = _dot(xn_s[HALO:HALO + ts, :], w_gate_ref[:, cs])
        v = (y * _silu(z)).astype(BF16)
        o_ref[0] += _dot(v, w_out_ref[cs, :])


def _sconv_kernel(xm_ref, xp_ref, xq_ref, g_ref, w_in_ref, conv_w_ref, w_out_ref, fg_ref, o_ref,
                  xn_s, ch_s):
    _fill_xn_ext(xm_ref, xp_ref, xq_ref, g_ref, xn_s)
    ts = xm_ref.shape[1]
    o_ref[0] = xm_ref[0]
    base = HALO - SHORT_CONV_WIDTH // 2

    for c in range(NC):
        cs = _cols(0, c)
        xe = xn_s[...]
        cg = _dot(xe, w_in_ref[:, _cols(1, c)])
        hh = _dot(xe, w_in_ref[:, _cols(2, c)])
        _store_lane_tiles(ch_s, c * LT, cg * hh)
        parts = []
        for lt in range(LT):
            slab = c * LT + lt
            ls = slice(c * CW + lt * LANES, c * CW + (lt + 1) * LANES)
            conv = ch_s[slab, base:base + ts, :] * conv_w_ref[0:1, ls]
            for k in range(1, SHORT_CONV_WIDTH):
                conv = conv + ch_s[slab, base + k:base + k + ts, :] * conv_w_ref[k:k + 1, ls]
            parts.append(conv)
        conv = jnp.concatenate(parts, axis=-1)
        xm = xn_s[HALO:HALO + ts, :]
        bg = _dot(xm, w_in_ref[:, cs])
        z = _dot(xm, w_in_ref[:, _cols(3, c)])
        v = (bg * conv * _silu(z)).astype(BF16)
        o_ref[0] += _dot(v, w_out_ref[cs, :])
    o_ref[0] = _rms(o_ref[0], fg_ref[...])


def _fnet_mix_weights_kernel(cc_ref, sc_ref, w_ref, o_ref):
    w = w_ref[0].astype(BF16)
    gd = FNET_GROUP_DIM
    o_ref[0, :, 0:gd] = _dot(cc_ref[...], w).astype(BF16)
    o_ref[0, :, gd:2 * gd] = _dot(sc_ref[...], w).astype(BF16)


def _fnet_in_kernel(x_ref, g_ref, w_in_ref, ab_ref, pq_ref, xn_s, pq_s):
    xn_s[...] = _rms(x_ref[0], g_ref[...]).astype(BF16)
    n1_rows = x_ref.shape[1] // FFT_N2
    slab_sets = pq_s.shape[0] // (2 * LT)

    for c in range(NC):
        first = (c % slab_sets) * 2 * LT
        u = _dot(xn_s[...], w_in_ref[:, _cols(0, c)]).astype(BF16)
        _store_lane_tiles(pq_s, first, _dot(u, ab_ref[c]))
        for n2 in range(FFT_N2):
            for part in range(2):
                for lt in range(LT):
                    piece = pq_s[first + part * LT + lt, pl.ds(n2, n1_rows, stride=FFT_N2), :]
                    pq_ref[0, c, n2, part, :, lt * LANES:(lt + 1) * LANES] = piece.astype(BF16)


def _fnet_seq_dft_kernel(m_ref, twc_ref, tws_ref, pq_ref, b_ref, f_ref, a_s):
    rt = 0.5 ** 0.5

    k1_half = FFT_N1 // 2

    def mxu_dot(half, rh, n2):
        hl = slice(half * FFT_HALF, (half + 1) * FFT_HALF)
        re = slice(rh * k1_half, (rh + 1) * k1_half)
        im = slice(FFT_N1 + rh * k1_half, FFT_N1 + (rh + 1) * k1_half)
        lhs = jnp.concatenate([m_ref[re, :], m_ref[im, :]], axis=0)
        res = _dot(lhs, pq_ref[0, 0, n2, :, :, hl].reshape(2 * FFT_N1, FFT_HALF))
        a_s[half, n2, re, :] = res[0:k1_half]
        a_s[half, n2, im, :] = res[k1_half:2 * k1_half]

    def row_block(half, rb):
        r0 = rb * FFT_ROWS
        re_rows = pl.ds(r0, FFT_ROWS)
        im_rows = pl.ds(FFT_N1 + r0, FFT_ROWS)
        for lt in range(FFT_HALF // LANES):
            ls = slice(lt * LANES, (lt + 1) * LANES)
            a = [a_s[half, 0, re_rows, ls]]
            b = [None]
            for n2 in range(1, FFT_N2):
                ar = a_s[half, n2, re_rows, ls]
                ai = a_s[half, n2, im_rows, ls]
                tc = twc_ref[n2, re_rows, :]
                ts = tws_ref[n2, re_rows, :]
                a.append(ar * tc + ai * ts)
                b.append(None if n2 == FFT_N2 // 2 else ai * tc - ar * ts)
            s04, d04 = a[0] + a[4], a[0] - a[4]
            p1, p2, p3 = a[1] + a[7], a[2] + a[6], a[3] + a[5]
            p13, t = p1 + p3, rt * (p1 - p3)
            e = s04 + p2
            q1, q2, q3 = b[1] - b[7], b[2] - b[6], b[3] - b[5]
            u = rt * (q1 + q3)
            ca = [e + p13, d04 + t, s04 - p2, d04 - t, e - p13]
            sb = [None, q2 + u, q1 - q3, u - q2]
            out = [ca[0], ca[1] + sb[1], ca[2] + sb[2], ca[3] + sb[3], ca[4],
                   ca[3] - sb[3], ca[2] - sb[2], ca[1] - sb[1]]
            os = slice(half * FFT_HALF + lt * LANES, half * FFT_HALF + (lt + 1) * LANES)
            bias = b_ref[0, :, os]
            for k2 in range(FFT_N2):
                rows = pl.ds(k2 * FFT_N1 + r0, FFT_ROWS)
                f_ref[0, 0, rows, os] = (out[k2] + bias).astype(BF16)

    stages = [(half, rh) for half in range(CW // FFT_HALF) for rh in range(2)]
    blocks_per_stage = k1_half // FFT_ROWS
    assert blocks_per_stage == FFT_N2
    for s in range(len(stages) + 1):
        for n2 in range(FFT_N2):
            if s < len(stages):
                mxu_dot(*stages[s], n2)
            if s > 0:
                p_half, p_rh = stages[s - 1]
                row_block(p_half, p_rh * blocks_per_stage + n2)


def _fnet_out_kernel(x_ref, g_ref, w_gate_ref, f_ref, w_out_ref, o_ref, xn_s):
    xn_s[...] = _rms(x_ref[0], g_ref[...]).astype(BF16)
    o_ref[0] = x_ref[0]
    for c in range(NC):
        z = _dot(xn_s[...], w_gate_ref[:, _cols(0, c)])
        v = (f_ref[0, c].astype(F32) * _silu(z)).astype(BF16)
        o_ref[0] += _dot(v, w_out_ref[_cols(0, c), :])


def _resident(shape):
    nd = len(shape)
    return pl.BlockSpec(shape, lambda *_: (0,) * nd, pipeline_mode=pl.Buffered(1))


def _params(sem):
    return pltpu.CompilerParams(dimension_semantics=sem, vmem_limit_bytes=VMEM_LIMIT)


def _row_tile(ts):
    return pl.BlockSpec((1, ts, D_MODEL), lambda b, i: (b, i, 0))


def _tiled_layer(body, name, x, consts, scratch, ts):
    per_tile = ts // HALO
    prev = pl.BlockSpec((1, HALO, D_MODEL), lambda b, i: (b, jnp.maximum(i * per_tile - 1, 0), 0))
    nxt = pl.BlockSpec((1, HALO, D_MODEL),
                       lambda b, i: (b, jnp.minimum((i + 1) * per_tile, SEQ // HALO - 1), 0))
    return pl.pallas_call(
        body,
        grid=(BATCH, SEQ // ts),
        in_specs=[_row_tile(ts), prev, nxt] + [_resident(c.shape) for c in consts],
        out_specs=_row_tile(ts),
        out_shape=jax.ShapeDtypeStruct((BATCH, SEQ, D_MODEL), F32),
        scratch_shapes=[pltpu.VMEM((ts + 2 * HALO, D_MODEL), BF16)] + scratch,
        compiler_params=_params(("arbitrary", "arbitrary")),
        name=name,
    )(x, x, x, *consts)


def _dft_tables(n):
    k = np.arange(n, dtype=np.int64)
    ang = ((k[:, None] * k[None, :]) % n) * (2.0 * np.pi / n)
    return np.cos(ang), np.sin(ang)


def _seq_fft_tables():
    c, s = _dft_tables(FFT_N1)
    m = np.block([[c, -s], [-s, -c]])
    n2 = np.arange(FFT_N2, dtype=np.int64)[:, None]
    k1 = np.arange(FFT_N1, dtype=np.int64)[None, :]
    ang = ((n2 * k1) % SEQ) * (2.0 * np.pi / SEQ)
    bcast = lambda t: np.broadcast_to(t[:, :, None], (FFT_N2, FFT_N1, LANES))
    return (jnp.asarray(m, dtype=BF16), jnp.asarray(bcast(np.cos(ang)), dtype=F32),
            jnp.asarray(bcast(np.sin(ang)), dtype=F32))


def _conv_dft_tables():
    n, hw = CONV_N, CONV_N // 2
    t = np.arange(n, dtype=np.int64)
    f = np.arange(hw, dtype=np.int64)
    ang = ((f[:, None] * t[None, :]) % n) * (2.0 * np.pi / n)
    fwd = np.concatenate([np.cos(ang), -np.sin(ang)], axis=0)
    fwd[hw] = np.cos(np.pi * t)
    inv = np.concatenate([np.cos(ang.T), -np.sin(ang.T)], axis=1) * (2.0 / n)
    inv[:, 0] = 1.0 / n
    inv[:, hw] = np.cos(np.pi * t) / n
    lag = CONF_KERNEL // 2 - np.arange(CONF_KERNEL, dtype=np.int64)
    angf = ((f[:, None] * lag[None, :]) % n) * (2.0 * np.pi / n)
    re_bot = np.cos(angf)
    re_bot[0] = np.cos(np.pi * lag)
    trig = np.zeros((n + hw, LANES))
    trig[:, :CONF_KERNEL] = np.concatenate([np.cos(angf), re_bot, -np.sin(angf)], axis=0)
    return jnp.asarray(fwd, dtype=BF16), jnp.asarray(inv, dtype=BF16), jnp.asarray(trig, dtype=F32)


def _fnet_layer(x, g, w_branch, w_gate, w_mix, b_mix, w_out):
    gd = FNET_GROUP_DIM
    ortho = 1.0 / (SEQ * gd) ** 0.5
    cc, sc = _dft_tables(gd)
    ab = pl.pallas_call(
        _fnet_mix_weights_kernel,
        grid=(FNET_GROUPS,),
        in_specs=[_resident((gd, gd)), _resident((gd, gd)),
                  pl.BlockSpec((1, gd, gd), lambda c: (c, 0, 0))],
        out_specs=pl.BlockSpec((1, gd, 2 * gd), lambda c: (c, 0, 0)),
        out_shape=jax.ShapeDtypeStruct((FNET_GROUPS, gd, 2 * gd), BF16),
        compiler_params=_params(("arbitrary",)),
        name="fnet_mix_weights",
    )(jnp.asarray(cc * ortho, dtype=BF16), jnp.asarray(sc * ortho, dtype=BF16), w_mix)

    ts = TS_WIDE
    pq = pl.pallas_call(
        _fnet_in_kernel,
        grid=(BATCH, SEQ // ts),
        in_specs=[_row_tile(ts), _resident(g.shape), _resident(w_branch.shape), _resident(ab.shape)],
        out_specs=pl.BlockSpec((1, NC, FFT_N2, 2, ts // FFT_N2, CW), lambda b, i: (b, 0, 0, 0, i, 0)),
        out_shape=jax.ShapeDtypeStruct((BATCH, NC, FFT_N2, 2, FFT_N1, CW), BF16),
        scratch_shapes=[pltpu.VMEM((ts, D_MODEL), BF16), pltpu.VMEM((2 * 2 * LT, ts, LANES), F32)],
        compiler_params=_params(("arbitrary", "arbitrary")),
        name="fnet_in",
    )(x, g, w_branch, ab)

    m, twc, tws = _seq_fft_tables()
    f = pl.pallas_call(
        _fnet_seq_dft_kernel,
        grid=(BATCH, NC),
        in_specs=[_resident(m.shape), _resident(twc.shape), _resident(tws.shape),
                  pl.BlockSpec((1, 1, FFT_N2, 2, FFT_N1, CW), lambda b, c: (b, c, 0, 0, 0, 0)),
                  pl.BlockSpec((1, 1, CW), lambda b, c: (c, 0, 0))],
        out_specs=pl.BlockSpec((1, 1, SEQ, CW), lambda b, c: (b, c, 0, 0)),
        out_shape=jax.ShapeDtypeStruct((BATCH, NC, SEQ, CW), BF16),
        scratch_shapes=[pltpu.VMEM((CW // FFT_HALF, FFT_N2, 2 * FFT_N1, FFT_HALF), F32)],
        compiler_params=_params(("arbitrary", "arbitrary")),
        name="fnet_seq_dft",
    )(m, twc, tws, pq, b_mix.reshape(NC, 1, CW))

    return pl.pallas_call(
        _fnet_out_kernel,
        grid=(BATCH, SEQ // ts),
        in_specs=[_row_tile(ts), _resident(g.shape), _resident(w_gate.shape),
                  pl.BlockSpec((1, NC, ts, CW), lambda b, i: (b, 0, i, 0)), _resident(w_out.shape)],
        out_specs=_row_tile(ts),
        out_shape=jax.ShapeDtypeStruct((BATCH, SEQ, D_MODEL), F32),
        scratch_shapes=[pltpu.VMEM((ts, D_MODEL), BF16)],
        compiler_params=_params(("arbitrary", "arbitrary")),
        name="fnet_out",
    )(x, g, w_gate, f, w_out)


def kernel(x, norm_g, w_out, final_g, fnet_w_in, fnet_w_mix, fnet_b_mix, conf_w_in, conf_dw_w, conf_dw_b, conf_ln_g, conf_ln_b, pool_w_in, pool_w_grp, pool_scale, sc_w_in, sc_conv_w):
    assert x.shape == (BATCH, SEQ, D_MODEL) and x.dtype == F32
    assert fnet_w_in.shape[0] == conf_w_in.shape[0] == pool_w_in.shape[0] == sc_w_in.shape[0] == 1
    g = norm_g.reshape(4, 1, D_MODEL)
    w_out_b = w_out.astype(BF16)
    slabs = pltpu.VMEM((NC * LT, TS_WIDE + 2 * HALO, LANES), F32)

    x = _fnet_layer(x, g[0], fnet_w_in[0, :, :D_INNER].astype(BF16), fnet_w_in[0, :, D_INNER:].astype(BF16),
                    fnet_w_mix[0], fnet_b_mix[0], w_out_b[0])

    fwd, inv, trig = _conv_dft_tables()
    dw_w = jnp.pad(conf_dw_w[0], ((0, LANES - CONF_KERNEL), (0, 0)))
    gtab = pl.pallas_call(
        _conf_filter_kernel,
        grid=(NC,),
        in_specs=[_resident(trig.shape), pl.BlockSpec((LANES, CW), lambda c: (0, c))],
        out_specs=pl.BlockSpec((trig.shape[0], CW), lambda c: (0, c)),
        out_shape=jax.ShapeDtypeStruct((trig.shape[0], D_INNER), F32),
        compiler_params=_params(("arbitrary",)),
        name="conformer_filter_spectrum",
    )(trig, dw_w)
    x = _tiled_layer(
        _conf_kernel, "conformer_layer", x,
        [g[1], conf_w_in[0].astype(BF16), fwd, inv, gtab, conf_dw_b, conf_ln_g, conf_ln_b, w_out_b[1]],
        [pltpu.VMEM((NC, TS, CW), F32)], TS)

    pool_w = pool_w_in[0].astype(BF16)
    w_fold = pl.pallas_call(
        _pool_fold_kernel,
        grid=(NC,),
        in_specs=[pl.BlockSpec((D_MODEL, CW), lambda c: (0, c)),
                  pl.BlockSpec((1, CW, CW), lambda c: (c, 0, 0))],
        out_specs=pl.BlockSpec((D_MODEL, CW), lambda c: (0, c)),
        out_shape=jax.ShapeDtypeStruct((D_MODEL, D_INNER), BF16),
        compiler_params=_params(("arbitrary",)),
        name="pool_fold_weights",
    )(pool_w, pool_w_grp[0])
    x = _tiled_layer(
        _pool_kernel, "pool_layer", x,
        [g[2], w_fold, pool_w[:, D_INNER:], pool_scale, w_out_b[2]],
        [slabs], TS_WIDE)

    x = _tiled_layer(
        _sconv_kernel, "short_conv_layer", x,
        [g[3], sc_w_in[0].astype(BF16), sc_conv_w[0], w_out_b[3], final_g.reshape(1, D_MODEL)],
        [slabs], TS_WIDE)
    return x
```

```python
import jax
import jax.numpy as jnp
import numpy as np
from jax import lax
from jax.experimental import pallas as pl
from jax.experimental.pallas import tpu as pltpu

D_MODEL = 1024
D_INNER = 2048
BATCH = 8
SEQ = 2048
FNET_GROUPS = 4
FNET_GROUP_DIM = D_INNER // FNET_GROUPS
CONF_KERNEL = 31
POOL_WINDOWS = (2, 4, 8, 16)
SHORT_CONV_WIDTH = 3
RMS_EPS = 1e-6
LN_EPS = 1e-5

TS = 512
TS_WIDE = 1024
HALO = 16
EXT = TS + 2 * HALO
CONV_N = 256
CONV_BLOCKS = ((0, 16, 240), (208, 32, 240), (288, 160, 240))
CW = 512
NC = D_INNER // CW
LANES = 128
LT = CW // LANES
FFT_N1 = 256
FFT_N2 = SEQ // FFT_N1
FFT_ROWS = 16
FFT_HALF = 256
VMEM_LIMIT = 56 * 1024 * 1024

F32 = jnp.float32
BF16 = jnp.bfloat16


def _dot(a, b):
    return jnp.dot(a, b, preferred_element_type=F32)


def _rms(v, g):
    ms = jnp.mean(v * v, axis=-1, keepdims=True)
    return v * lax.rsqrt(ms + RMS_EPS) * g


def _silu(v):
    return v * jax.nn.sigmoid(v)


def _cols(part, c):
    start = part * D_INNER + c * CW
    return slice(start, start + CW)


def _fill_xn_ext(xm_ref, xp_ref, xq_ref, g_ref, xn_s):
    i = pl.program_id(1)
    ts = xm_ref.shape[1]
    g = g_ref[...]
    prev = jnp.where(i > 0, _rms(xp_ref[0], g), 0.0)
    nxt = jnp.where(i < pl.num_programs(1) - 1, _rms(xq_ref[0], g), 0.0)
    xn_s[0:HALO, :] = prev.astype(BF16)
    xn_s[HALO:HALO + ts, :] = _rms(xm_ref[0], g).astype(BF16)
    xn_s[HALO + ts:ts + 2 * HALO, :] = nxt.astype(BF16)


def _store_lane_tiles(slab_ref, first, val):
    for lt in range(val.shape[-1] // LANES):
        slab_ref[first + lt] = val[:, lt * LANES:(lt + 1) * LANES]


def _conf_filter_kernel(trig_ref, w_ref, o_ref):
    o_ref[...] = jnp.dot(trig_ref[...], w_ref[...], preferred_element_type=F32,
                         precision=lax.Precision.HIGHEST)


def _conf_kernel(xm_ref, xp_ref, xq_ref, g_ref, w_in_ref, fwd_ref, inv_ref, gtab_ref, dw_b_ref,
                 ln_g_ref, ln_b_ref, w_out_ref, o_ref, xn_s, hc_s):
    _fill_xn_ext(xm_ref, xp_ref, xq_ref, g_ref, xn_s)
    hw = CONV_N // 2

    for c in range(NC):
        cs = _cols(0, c)
        xe = xn_s[...]
        a = _dot(xe, w_in_ref[:, _cols(0, c)])
        gate = _dot(xe, w_in_ref[:, _cols(1, c)])
        h = (a * jax.nn.sigmoid(gate)).astype(BF16)
        nb = len(CONV_BLOCKS)
        tile3 = lambda v: jnp.concatenate([v] * nb, axis=-1)
        g_re_top = tile3(gtab_ref[0:hw, cs])
        g_re_bot = tile3(gtab_ref[hw:CONV_N, cs])
        g_im = tile3(gtab_ref[CONV_N:CONV_N + hw, cs])
        bias = dw_b_ref[:, cs]
        hb = jnp.concatenate([h[off:off + CONV_N] for off, _, _ in CONV_BLOCKS], axis=-1)
        spec = _dot(fwd_ref[...], hb)
        top, bot = spec[0:hw], spec[hw:CONV_N]
        y = jnp.concatenate([top * g_re_top - bot * g_im, top * g_im + bot * g_re_bot], axis=0)
        conv = _dot(inv_ref[...], y.astype(BF16))
        for j, (off, lo, hi) in enumerate(CONV_BLOCKS):
            t0 = off + lo - HALO
            hc_s[c, t0:t0 + hi - lo, :] = conv[lo:hi, j * CW:(j + 1) * CW] + bias

    tot = jnp.sum(hc_s[0], axis=-1, keepdims=True)
    for c in range(1, NC):
        tot = tot + jnp.sum(hc_s[c], axis=-1, keepdims=True)
    mu = tot * (1.0 / D_INNER)
    sq = jnp.sum(jnp.square(hc_s[0] - mu), axis=-1, keepdims=True)
    for c in range(1, NC):
        sq = sq + jnp.sum(jnp.square(hc_s[c] - mu), axis=-1, keepdims=True)
    rstd = lax.rsqrt(sq * (1.0 / D_INNER) + LN_EPS)

    o_ref[0] = xm_ref[0]
    for c in range(NC):
        cs = _cols(0, c)
        hn = (hc_s[c] - mu) * rstd * ln_g_ref[:, cs] + ln_b_ref[:, cs]
        z = _dot(xn_s[HALO:HALO + TS, :], w_in_ref[:, _cols(2, c)])
        v = (_silu(hn) * _silu(z)).astype(BF16)
        o_ref[0] += _dot(v, w_out_ref[cs, :])


def _pool_fold_kernel(w_in_ref, w_grp_ref, o_ref):
    o_ref[...] = _dot(w_in_ref[...], w_grp_ref[0].astype(BF16)).astype(BF16)


def _pool_kernel(xm_ref, xp_ref, xq_ref, g_ref, w_fold_ref, w_gate_ref, scale_ref, w_out_ref, o_ref,
                 xn_s, u_s):
    _fill_xn_ext(xm_ref, xp_ref, xq_ref, g_ref, xn_s)
    ts = xm_ref.shape[1]
    t = pl.program_id(1) * ts + lax.broadcasted_iota(jnp.int32, (ts, 1), 0)
    o_ref[0] = xm_ref[0]

    for c, w in enumerate(POOL_WINDOWS):
        cs = _cols(0, c)
        left = w // 2
        right = w - 1 - left
        _store_lane_tiles(u_s, c * LT, _dot(xn_s[...], w_fold_ref[:, cs]))
        cnt = jnp.minimum(t + right + 1, SEQ) - jnp.maximum(t - left, 0)
        inv_cnt = 1.0 / cnt.astype(F32)
        parts = []
        for lt in range(LT):
            slab = c * LT + lt
            wsum = u_s[slab, HALO - left:HALO - left + ts, :]
            for k in range(1, w):
                wsum = wsum + u_s[slab, HALO - left + k:HALO - left + k + ts, :]
            parts.append(wsum * inv_cnt - u_s[slab, HALO:HALO + ts, :])
        y = jnp.concatenate(parts, axis=-1) * scale_ref[:, cs]
        z = _dot(xn_s[HALO:HALO + ts, :], w_gate_ref[:, cs])
        v = (y * _silu(z)).astype(BF16)
        o_ref[0] += _dot(v, w_out_ref[cs, :])


def _sconv_kernel(xm_ref, xp_ref, xq_ref, g_ref, w_in_ref, conv_w_ref, w_out_ref, fg_ref, o_ref,
                  xn_s, ch_s):
    _fill_xn_ext(xm_ref, xp_ref, xq_ref, g_ref, xn_s)
    ts = xm_ref.shape[1]
    o_ref[0] = xm_ref[0]
    base = HALO - SHORT_CONV_WIDTH // 2

    for c in range(NC):
        cs = _cols(0, c)
        xe = xn_s[...]
        cg = _dot(xe, w_in_ref[:, _cols(1, c)])
        hh = _dot(xe, w_in_ref[:, _cols(2, c)])
        _store_lane_tiles(ch_s, c * LT, cg * hh)
        parts = []
        for lt in range(LT):
            slab = c * LT + lt
            ls = slice(c * CW + lt * LANES, c * CW + (lt + 1) * LANES)
            conv = ch_s[slab, base:base + ts, :] * conv_w_ref[0:1, ls]
            for k in range(1, SHORT_CONV_WIDTH):
                conv = conv + ch_s[slab, base + k:base + k + ts, :] * conv_w_ref[k:k + 1, ls]
            parts.append(conv)
        conv = jnp.concatenate(parts, axis=-1)
        xm = xn_s[HALO:HALO + ts, :]
        bg = _dot(xm, w_in_ref[:, cs])
        z = _dot(xm, w_in_ref[:, _cols(3, c)])
        v = (bg * conv * _silu(z)).astype(BF16)
        o_ref[0] += _dot(v, w_out_ref[cs, :])
    o_ref[0] = _rms(o_ref[0], fg_ref[...])


def _fnet_mix_weights_kernel(cc_ref, sc_ref, w_ref, o_ref):
    w = w_ref[0].astype(BF16)
    gd = FNET_GROUP_DIM
    o_ref[0, :, 0:gd] = _dot(cc_ref[...], w).astype(BF16)
    o_ref[0, :, gd:2 * gd] = _dot(sc_ref[...], w).astype(BF16)


def _fnet_in_kernel(x_ref, g_ref, w_in_ref, ab_ref, pq_ref, xn_s, pq_s):
    xn_s[...] = _rms(x_ref[0], g_ref[...]).astype(BF16)
    n1_rows = x_ref.shape[1] // FFT_N2
    slab_sets = pq_s.shape[0] // (2 * LT)

    for c in range(NC):
        first = (c % slab_sets) * 2 * LT
        u = _dot(xn_s[...], w_in_ref[:, _cols(0, c)]).astype(BF16)
        _store_lane_tiles(pq_s, first, _dot(u, ab_ref[c]))
        for n2 in range(FFT_N2):
            for part in range(2):
                for lt in range(LT):
                    piece = pq_s[first + part * LT + lt, pl.ds(n2, n1_rows, stride=FFT_N2), :]
                    pq_ref[0, c, n2, part, :, lt * LANES:(lt + 1) * LANES] = piece.astype(BF16)


def _fnet_seq_dft_kernel(m_ref, pq_ref, b_ref, f_ref, a_s):
    rt = 0.5 ** 0.5

    k1_half = FFT_N1 // 2

    def mxu_dot(half, rh, n2):
        hl = slice(half * FFT_HALF, (half + 1) * FFT_HALF)
        re = slice(rh * k1_half, (rh + 1) * k1_half)
        im = slice(FFT_N1 + rh * k1_half, FFT_N1 + (rh + 1) * k1_half)
        rhs = pq_ref[0, 0, n2, :, :, hl].reshape(2 * FFT_N1, FFT_HALF)
        if n2 % (FFT_N2 // 2) == 0:
            a_s[half, n2, re, :] = _dot(m_ref[n2, re, :], rhs)
            return
        res = _dot(jnp.concatenate([m_ref[n2, re, :], m_ref[n2, im, :]], axis=0), rhs)
        a_s[half, n2, re, :] = res[0:k1_half]
        a_s[half, n2, im, :] = res[k1_half:2 * k1_half]

    def row_block(half, rb):
        r0 = rb * FFT_ROWS
        re_rows = pl.ds(r0, FFT_ROWS)
        im_rows = pl.ds(FFT_N1 + r0, FFT_ROWS)
        for lt in range(FFT_HALF // LANES):
            ls = slice(lt * LANES, (lt + 1) * LANES)
            a = [a_s[half, n2, re_rows, ls] for n2 in range(FFT_N2)]
            b = [None if n2 % (FFT_N2 // 2) == 0 else a_s[half, n2, im_rows, ls] for n2 in range(FFT_N2)]
            s04, d04 = a[0] + a[4], a[0] - a[4]
            p1, p2, p3 = a[1] + a[7], a[2] + a[6], a[3] + a[5]
            p13, t = p1 + p3, rt * (p1 - p3)
            e = s04 + p2
            q1, q2, q3 = b[1] - b[7], b[2] - b[6], b[3] - b[5]
            u = rt * (q1 + q3)
            ca = [e + p13, d04 + t, s04 - p2, d04 - t, e - p13]
            sb = [None, q2 + u, q1 - q3, u - q2]
            out = [ca[0], ca[1] + sb[1], ca[2] + sb[2], ca[3] + sb[3], ca[4],
                   ca[3] - sb[3], ca[2] - sb[2], ca[1] - sb[1]]
            os = slice(half * FFT_HALF + lt * LANES, half * FFT_HALF + (lt + 1) * LANES)
            bias = b_ref[0, :, os]
            for k2 in range(FFT_N2):
                rows = pl.ds(k2 * FFT_N1 + r0, FFT_ROWS)
                f_ref[0, 0, rows, os] = (out[k2] + bias).astype(BF16)

    stages = [(half, rh) for half in range(CW // FFT_HALF) for rh in range(2)]
    blocks_per_stage = k1_half // FFT_ROWS
    assert blocks_per_stage == FFT_N2
    for s in range(len(stages) + 1):
        for n2 in range(FFT_N2):
            if s < len(stages):
                mxu_dot(*stages[s], n2)
            if s > 0:
                p_half, p_rh = stages[s - 1]
                row_block(p_half, p_rh * blocks_per_stage + n2)


def _fnet_out_kernel(x_ref, g_ref, w_gate_ref, f_ref, w_out_ref, o_ref, xn_s):
    xn_s[...] = _rms(x_ref[0], g_ref[...]).astype(BF16)
    o_ref[0] = x_ref[0]
    for c in range(NC):
        z = _dot(xn_s[...], w_gate_ref[:, _cols(0, c)])
        v = (f_ref[0, c].astype(F32) * _silu(z)).astype(BF16)
        o_ref[0] += _dot(v, w_out_ref[_cols(0, c), :])


def _resident(shape):
    nd = len(shape)
    return pl.BlockSpec(shape, lambda *_: (0,) * nd, pipeline_mode=pl.Buffered(1))


def _params(sem):
    return pltpu.CompilerParams(dimension_semantics=sem, vmem_limit_bytes=VMEM_LIMIT)


def _row_tile(ts):
    return pl.BlockSpec((1, ts, D_MODEL), lambda b, i: (b, i, 0))


def _tiled_layer(body, name, x, consts, scratch, ts):
    per_tile = ts // HALO
    prev = pl.BlockSpec((1, HALO, D_MODEL), lambda b, i: (b, jnp.maximum(i * per_tile - 1, 0), 0))
    nxt = pl.BlockSpec((1, HALO, D_MODEL),
                       lambda b, i: (b, jnp.minimum((i + 1) * per_tile, SEQ // HALO - 1), 0))
    return pl.pallas_call(
        body,
        grid=(BATCH, SEQ // ts),
        in_specs=[_row_tile(ts), prev, nxt] + [_resident(c.shape) for c in consts],
        out_specs=_row_tile(ts),
        out_shape=jax.ShapeDtypeStruct((BATCH, SEQ, D_MODEL), F32),
        scratch_shapes=[pltpu.VMEM((ts + 2 * HALO, D_MODEL), BF16)] + scratch,
        compiler_params=_params(("arbitrary", "arbitrary")),
        name=name,
    )(x, x, x, *consts)


def _dft_tables(n):
    k = np.arange(n, dtype=np.int64)
    ang = ((k[:, None] * k[None, :]) % n) * (2.0 * np.pi / n)
    return np.cos(ang), np.sin(ang)


def _seq_fft_tables():
    k1 = np.arange(FFT_N1, dtype=np.int64)[:, None]
    n1 = np.arange(FFT_N1, dtype=np.int64)[None, :]
    mats = []
    for n2 in range(FFT_N2):
        ang = ((k1 * (FFT_N2 * n1 + n2)) % SEQ) * (2.0 * np.pi / SEQ)
        c, s = np.cos(ang), np.sin(ang)
        mats.append(np.block([[c, -s], [-s, -c]]))
    return jnp.asarray(np.stack(mats), dtype=BF16)


def _conv_dft_tables():
    n, hw = CONV_N, CONV_N // 2
    t = np.arange(n, dtype=np.int64)
    f = np.arange(hw, dtype=np.int64)
    ang = ((f[:, None] * t[None, :]) % n) * (2.0 * np.pi / n)
    fwd = np.concatenate([np.cos(ang), -np.sin(ang)], axis=0)
    fwd[hw] = np.cos(np.pi * t)
    inv = np.concatenate([np.cos(ang.T), -np.sin(ang.T)], axis=1) * (2.0 / n)
    inv[:, 0] = 1.0 / n
    inv[:, hw] = np.cos(np.pi * t) / n
    lag = CONF_KERNEL // 2 - np.arange(CONF_KERNEL, dtype=np.int64)
    angf = ((f[:, None] * lag[None, :]) % n) * (2.0 * np.pi / n)
    re_bot = np.cos(angf)
    re_bot[0] = np.cos(np.pi * lag)
    trig = np.zeros((n + hw, LANES))
    trig[:, :CONF_KERNEL] = np.concatenate([np.cos(angf), re_bot, -np.sin(angf)], axis=0)
    return jnp.asarray(fwd, dtype=BF16), jnp.asarray(inv, dtype=BF16), jnp.asarray(trig, dtype=F32)


def _fnet_layer(x, g, w_branch, w_gate, w_mix, b_mix, w_out):
    gd = FNET_GROUP_DIM
    ortho = 1.0 / (SEQ * gd) ** 0.5
    cc, sc = _dft_tables(gd)
    ab = pl.pallas_call(
        _fnet_mix_weights_kernel,
        grid=(FNET_GROUPS,),
        in_specs=[_resident((gd, gd)), _resident((gd, gd)),
                  pl.BlockSpec((1, gd, gd), lambda c: (c, 0, 0))],
        out_specs=pl.BlockSpec((1, gd, 2 * gd), lambda c: (c, 0, 0)),
        out_shape=jax.ShapeDtypeStruct((FNET_GROUPS, gd, 2 * gd), BF16),
        compiler_params=_params(("arbitrary",)),
        name="fnet_mix_weights",
    )(jnp.asarray(cc * ortho, dtype=BF16), jnp.asarray(sc * ortho, dtype=BF16), w_mix)

    ts = TS_WIDE
    pq = pl.pallas_call(
        _fnet_in_kernel,
        grid=(BATCH, SEQ // ts),
        in_specs=[_row_tile(ts), _resident(g.shape), _resident(w_branch.shape), _resident(ab.shape)],
        out_specs=pl.BlockSpec((1, NC, FFT_N2, 2, ts // FFT_N2, CW), lambda b, i: (b, 0, 0, 0, i, 0)),
        out_shape=jax.ShapeDtypeStruct((BATCH, NC, FFT_N2, 2, FFT_N1, CW), BF16),
        scratch_shapes=[pltpu.VMEM((ts, D_MODEL), BF16), pltpu.VMEM((2 * 2 * LT, ts, LANES), F32)],
        compiler_params=_params(("arbitrary", "arbitrary")),
        name="fnet_in",
    )(x, g, w_branch, ab)

    m = _seq_fft_tables()
    f = pl.pallas_call(
        _fnet_seq_dft_kernel,
        grid=(BATCH, NC),
        in_specs=[_resident(m.shape),
                  pl.BlockSpec((1, 1, FFT_N2, 2, FFT_N1, CW), lambda b, c: (b, c, 0, 0, 0, 0)),
                  pl.BlockSpec((1, 1, CW), lambda b, c: (c, 0, 0))],
        out_specs=pl.BlockSpec((1, 1, SEQ, CW), lambda b, c: (b, c, 0, 0)),
        out_shape=jax.ShapeDtypeStruct((BATCH, NC, SEQ, CW), BF16),
        scratch_shapes=[pltpu.VMEM((CW // FFT_HALF, FFT_N2, 2 * FFT_N1, FFT_HALF), F32)],
        compiler_params=_params(("arbitrary", "arbitrary")),
        name="fnet_seq_dft",
    )(m, pq, b_mix.reshape(NC, 1, CW))

    return pl.pallas_call(
        _fnet_out_kernel,
        grid=(BATCH, SEQ // ts),
        in_specs=[_row_tile(ts), _resident(g.shape), _resident(w_gate.shape),
                  pl.BlockSpec((1, NC, ts, CW), lambda b, i: (b, 0, i, 0)), _resident(w_out.shape)],
        out_specs=_row_tile(ts),
        out_shape=jax.ShapeDtypeStruct((BATCH, SEQ, D_MODEL), F32),
        scratch_shapes=[pltpu.VMEM((ts, D_MODEL), BF16)],
        compiler_params=_params(("arbitrary", "arbitrary")),
        name="fnet_out",
    )(x, g, w_gate, f, w_out)


def kernel(x, norm_g, w_out, final_g, fnet_w_in, fnet_w_mix, fnet_b_mix, conf_w_in, conf_dw_w, conf_dw_b, conf_ln_g, conf_ln_b, pool_w_in, pool_w_grp, pool_scale, sc_w_in, sc_conv_w):
    assert x.shape == (BATCH, SEQ, D_MODEL) and x.dtype == F32
    assert fnet_w_in.shape[0] == conf_w_in.shape[0] == pool_w_in.shape[0] == sc_w_in.shape[0] == 1
    g = norm_g.reshape(4, 1, D_MODEL)
    w_out_b = w_out.astype(BF16)
    slabs = pltpu.VMEM((NC * LT, TS_WIDE + 2 * HALO, LANES), F32)

    x = _fnet_layer(x, g[0], fnet_w_in[0, :, :D_INNER].astype(BF16), fnet_w_in[0, :, D_INNER:].astype(BF16),
                    fnet_w_mix[0], fnet_b_mix[0], w_out_b[0])

    fwd, inv, trig = _conv_dft_tables()
    dw_w = jnp.pad(conf_dw_w[0], ((0, LANES - CONF_KERNEL), (0, 0)))
    gtab = pl.pallas_call(
        _conf_filter_kernel,
        grid=(NC,),
        in_specs=[_resident(trig.shape), pl.BlockSpec((LANES, CW), lambda c: (0, c))],
        out_specs=pl.BlockSpec((trig.shape[0], CW), lambda c: (0, c)),
        out_shape=jax.ShapeDtypeStruct((trig.shape[0], D_INNER), F32),
        compiler_params=_params(("arbitrary",)),
        name="conformer_filter_spectrum",
    )(trig, dw_w)
    x = _tiled_layer(
        _conf_kernel, "conformer_layer", x,
        [g[1], conf_w_in[0].astype(BF16), fwd, inv, gtab, conf_dw_b, conf_ln_g, conf_ln_b, w_out_b[1]],
        [pltpu.VMEM((NC, TS, CW), F32)], TS)

    pool_w = pool_w_in[0].astype(BF16)
    w_fold = pl.pallas_call(
        _pool_fold_kernel,
        grid=(NC,),
        in_specs=[pl.BlockSpec((D_MODEL, CW), lambda c: (0, c)),
                  pl.BlockSpec((1, CW, CW), lambda c: (c, 0, 0))],
        out_specs=pl.BlockSpec((D_MODEL, CW), lambda c: (0, c)),
        out_shape=jax.ShapeDtypeStruct((D_MODEL, D_INNER), BF16),
        compiler_params=_params(("arbitrary",)),
        name="pool_fold_weights",
    )(pool_w, pool_w_grp[0])
    x = _tiled_layer(
        _pool_kernel, "pool_layer", x,
        [g[2], w_fold, pool_w[:, D_INNER:], pool_scale, w_out_b[2]],
        [slabs], TS_WIDE)

    x = _tiled_layer(
        _sconv_kernel, "short_conv_layer", x,
        [g[3], sc_w_in[0].astype(BF16), sc_conv_w[0], w_out_b[3], final_g.reshape(1, D_MODEL)],
        [slabs], TS_WIDE)
    return x
```

```python
import jax
import jax.numpy as jnp
import numpy as np
from jax import lax
from jax.experimental import pallas as pl
from jax.experimental.pallas import tpu as pltpu

D_MODEL = 1024
D_INNER = 2048
BATCH = 8
SEQ = 2048
FNET_GROUPS = 4
FNET_GROUP_DIM = D_INNER // FNET_GROUPS
CONF_KERNEL = 31
POOL_WINDOWS = (2, 4, 8, 16)
SHORT_CONV_WIDTH = 3
RMS_EPS = 1e-6
LN_EPS = 1e-5

TS = 512
TS_WIDE = 1024
HALO = 16
EXT = TS + 2 * HALO
CONV_N = 256
CONV_BLOCKS = ((0, 16, 240), (208, 32, 240), (288, 160, 240))
CW = 512
NC = D_INNER // CW
LANES = 128
LT = CW // LANES
FFT_N1 = 256
FFT_N2 = SEQ // FFT_N1
FFT_ROWS = 16
FFT_HALF = 256
VMEM_LIMIT = 56 * 1024 * 1024

F32 = jnp.float32
BF16 = jnp.bfloat16


def _dot(a, b):
    return jnp.dot(a, b, preferred_element_type=F32)


def _rms(v, g):
    ms = jnp.mean(v * v, axis=-1, keepdims=True)
    return v * lax.rsqrt(ms + RMS_EPS) * g


def _silu(v):
    return v * jax.nn.sigmoid(v)


def _cols(part, c):
    start = part * D_INNER + c * CW
    return slice(start, start + CW)


def _fill_xn_ext(xm_ref, xp_ref, xq_ref, g_ref, xn_s):
    i = pl.program_id(1)
    ts = xm_ref.shape[1]
    g = g_ref[...]
    prev = jnp.where(i > 0, _rms(xp_ref[0], g), 0.0)
    nxt = jnp.where(i < pl.num_programs(1) - 1, _rms(xq_ref[0], g), 0.0)
    xn_s[0:HALO, :] = prev.astype(BF16)
    xn_s[HALO:HALO + ts, :] = _rms(xm_ref[0], g).astype(BF16)
    xn_s[HALO + ts:ts + 2 * HALO, :] = nxt.astype(BF16)


def _store_lane_tiles(slab_ref, first, val):
    for lt in range(val.shape[-1] // LANES):
        slab_ref[first + lt] = val[:, lt * LANES:(lt + 1) * LANES]


def _conf_filter_kernel(trig_ref, w_ref, o_ref):
    o_ref[...] = jnp.dot(trig_ref[...], w_ref[...], preferred_element_type=F32,
                         precision=lax.Precision.HIGHEST)


def _conf_kernel(xm_ref, xp_ref, xq_ref, g_ref, w_in_ref, fwd_ref, inv_ref, gtab_ref, dw_b_ref,
                 ln_g_ref, ln_b_ref, w_out_ref, o_ref, xn_s, hc_s):
    _fill_xn_ext(xm_ref, xp_ref, xq_ref, g_ref, xn_s)
    hw = CONV_N // 2

    for c in range(NC):
        cs = _cols(0, c)
        xe = xn_s[...]
        a = _dot(xe, w_in_ref[:, _cols(0, c)])
        gate = _dot(xe, w_in_ref[:, _cols(1, c)])
        h = (a * jax.nn.sigmoid(gate)).astype(BF16)
        nb = len(CONV_BLOCKS)
        tile3 = lambda v: jnp.concatenate([v] * nb, axis=-1)
        g_re_top = tile3(gtab_ref[0:hw, cs])
        g_re_bot = tile3(gtab_ref[hw:CONV_N, cs])
        g_im = tile3(gtab_ref[CONV_N:CONV_N + hw, cs])
        bias = dw_b_ref[:, cs]
        hb = jnp.concatenate([h[off:off + CONV_N] for off, _, _ in CONV_BLOCKS], axis=-1)
        spec = _dot(fwd_ref[...], hb)
        top, bot = spec[0:hw], spec[hw:CONV_N]
        y = jnp.concatenate([top * g_re_top - bot * g_im, top * g_im + bot * g_re_bot], axis=0)
        conv = _dot(inv_ref[...], y.astype(BF16))
        for j, (off, lo, hi) in enumerate(CONV_BLOCKS):
            t0 = off + lo - HALO
            hc_s[c, t0:t0 + hi - lo, :] = conv[lo:hi, j * CW:(j + 1) * CW] + bias

    tot = jnp.sum(hc_s[0], axis=-1, keepdims=True)
    for c in range(1, NC):
        tot = tot + jnp.sum(hc_s[c], axis=-1, keepdims=True)
    mu = tot * (1.0 / D_INNER)
    sq = jnp.sum(jnp.square(hc_s[0] - mu), axis=-1, keepdims=True)
    for c in range(1, NC):
        sq = sq + jnp.sum(jnp.square(hc_s[c] - mu), axis=-1, keepdims=True)
    rstd = lax.rsqrt(sq * (1.0 / D_INNER) + LN_EPS)

    o_ref[0] = xm_ref[0]
    for c in range(NC):
        cs = _cols(0, c)
        hn = (hc_s[c] - mu) * rstd * ln_g_ref[:, cs] + ln_b_ref[:, cs]
        z = _dot(xn_s[HALO:HALO + TS, :], w_in_ref[:, _cols(2, c)])
        v = (_silu(hn) * _silu(z)).astype(BF16)
        o_ref[0] += _dot(v, w_out_ref[cs, :])


def _pool_fold_kernel(w_in_ref, w_grp_ref, o_ref):
    for c in range(NC):
        cs = _cols(0, c)
        o_ref[:, cs] = _dot(w_in_ref[:, cs], w_grp_ref[c].astype(BF16)).astype(BF16)


def _pool_kernel(xm_ref, xp_ref, xq_ref, g_ref, w_fold_ref, w_gate_ref, scale_ref, w_out_ref, o_ref,
                 xn_s, u_s):
    _fill_xn_ext(xm_ref, xp_ref, xq_ref, g_ref, xn_s)
    ts = xm_ref.shape[1]
    t = pl.program_id(1) * ts + lax.broadcasted_iota(jnp.int32, (ts, 1), 0)
    o_ref[0] = xm_ref[0]

    for c, w in enumerate(POOL_WINDOWS):
        cs = _cols(0, c)
        left = w // 2
        right = w - 1 - left
        _store_lane_tiles(u_s, c * LT, _dot(xn_s[...], w_fold_ref[:, cs]))
        cnt = jnp.minimum(t + right + 1, SEQ) - jnp.maximum(t - left, 0)
        inv_cnt = 1.0 / cnt.astype(F32)
        parts = []
        for lt in range(LT):
            slab = c * LT + lt
            wsum = u_s[slab, HALO - left:HALO - left + ts, :]
            for k in range(1, w):
                wsum = wsum + u_s[slab, HALO - left + k:HALO - left + k + ts, :]
            parts.append(wsum * inv_cnt - u_s[slab, HALO:HALO + ts, :])
        y = jnp.concatenate(parts, axis=-1) * scale_ref[:, cs]
        z = _dot(xn_s[HALO:HALO + ts, :], w_gate_ref[:, cs])
        v = (y * _silu(z)).astype(BF16)
        o_ref[0] += _dot(v, w_out_ref[cs, :])


def _sconv_kernel(xm_ref, xp_ref, xq_ref, g_ref, w_in_ref, conv_w_ref, w_out_ref, fg_ref, o_ref,
                  xn_s, ch_s):
    _fill_xn_ext(xm_ref, xp_ref, xq_ref, g_ref, xn_s)
    ts = xm_ref.shape[1]
    o_ref[0] = xm_ref[0]
    base = HALO - SHORT_CONV_WIDTH // 2

    for c in range(NC):
        cs = _cols(0, c)
        xe = xn_s[...]
        cg = _dot(xe, w_in_ref[:, _cols(1, c)])
        hh = _dot(xe, w_in_ref[:, _cols(2, c)])
        _store_lane_tiles(ch_s, c * LT, cg * hh)
        parts = []
        for lt in range(LT):
            slab = c * LT + lt
            ls = slice(c * CW + lt * LANES, c * CW + (lt + 1) * LANES)
            conv = ch_s[slab, base:base + ts, :] * conv_w_ref[0:1, ls]
            for k in range(1, SHORT_CONV_WIDTH):
                conv = conv + ch_s[slab, base + k:base + k + ts, :] * conv_w_ref[k:k + 1, ls]
            parts.append(conv)
        conv = jnp.concatenate(parts, axis=-1)
        xm = xn_s[HALO:HALO + ts, :]
        bg = _dot(xm, w_in_ref[:, cs])
        z = _dot(xm, w_in_ref[:, _cols(3, c)])
        v = (bg * conv * _silu(z)).astype(BF16)
        o_ref[0] += _dot(v, w_out_ref[cs, :])
    o_ref[0] = _rms(o_ref[0], fg_ref[...])


def _fnet_mix_weights_kernel(cc_ref, sc_ref, w_ref, o_ref):
    gd = FNET_GROUP_DIM
    for grp in range(FNET_GROUPS):
        w = w_ref[grp].astype(BF16)
        o_ref[grp, :, 0:gd] = _dot(cc_ref[...], w).astype(BF16)
        o_ref[grp, :, gd:2 * gd] = _dot(sc_ref[...], w).astype(BF16)


def _fnet_in_kernel(x_ref, g_ref, w_in_ref, ab_ref, pq_ref, xn_s, pq_s):
    xn_s[...] = _rms(x_ref[0], g_ref[...]).astype(BF16)
    n1_rows = x_ref.shape[1] // FFT_N2
    slab_sets = pq_s.shape[0] // (2 * LT)

    for c in range(NC):
        first = (c % slab_sets) * 2 * LT
        u = _dot(xn_s[...], w_in_ref[:, _cols(0, c)]).astype(BF16)
        _store_lane_tiles(pq_s, first, _dot(u, ab_ref[c]))
        for n2 in range(FFT_N2):
            for part in range(2):
                for lt in range(LT):
                    piece = pq_s[first + part * LT + lt, pl.ds(n2, n1_rows, stride=FFT_N2), :]
                    pq_ref[0, c, n2, part, :, lt * LANES:(lt + 1) * LANES] = piece.astype(BF16)


def _fnet_seq_dft_kernel(m_ref, pq_ref, b_ref, f_ref, a_s):
    rt = 0.5 ** 0.5

    k1_half = FFT_N1 // 2

    def mxu_dot(half, rh, n2):
        hl = slice(half * FFT_HALF, (half + 1) * FFT_HALF)
        re = slice(rh * k1_half, (rh + 1) * k1_half)
        im = slice(FFT_N1 + rh * k1_half, FFT_N1 + (rh + 1) * k1_half)
        rhs = pq_ref[0, 0, n2, :, :, hl].reshape(2 * FFT_N1, FFT_HALF)
        if n2 % (FFT_N2 // 2) == 0:
            a_s[half, n2, re, :] = _dot(m_ref[n2, re, :], rhs)
            return
        res = _dot(jnp.concatenate([m_ref[n2, re, :], m_ref[n2, im, :]], axis=0), rhs)
        a_s[half, n2, re, :] = res[0:k1_half]
        a_s[half, n2, im, :] = res[k1_half:2 * k1_half]

    def row_block(half, rb):
        r0 = rb * FFT_ROWS
        re_rows = pl.ds(r0, FFT_ROWS)
        im_rows = pl.ds(FFT_N1 + r0, FFT_ROWS)
        for lt in range(FFT_HALF // LANES):
            ls = slice(lt * LANES, (lt + 1) * LANES)
            a = [a_s[half, n2, re_rows, ls] for n2 in range(FFT_N2)]
            b = [None if n2 % (FFT_N2 // 2) == 0 else a_s[half, n2, im_rows, ls] for n2 in range(FFT_N2)]
            s04, d04 = a[0] + a[4], a[0] - a[4]
            p1, p2, p3 = a[1] + a[7], a[2] + a[6], a[3] + a[5]
            p13, t = p1 + p3, rt * (p1 - p3)
            e = s04 + p2
            q1, q2, q3 = b[1] - b[7], b[2] - b[6], b[3] - b[5]
            u = rt * (q1 + q3)
            ca = [e + p13, d04 + t, s04 - p2, d04 - t, e - p13]
            sb = [None, q2 + u, q1 - q3, u - q2]
            out = [ca[0], ca[1] + sb[1], ca[2] + sb[2], ca[3] + sb[3], ca[4],
                   ca[3] - sb[3], ca[2] - sb[2], ca[1] - sb[1]]
            os = slice(half * FFT_HALF + lt * LANES, half * FFT_HALF + (lt + 1) * LANES)
            bias = b_ref[0, :, os]
            for k2 in range(FFT_N2):
                rows = pl.ds(k2 * FFT_N1 + r0, FFT_ROWS)
                f_ref[0, 0, rows, os] = (out[k2] + bias).astype(BF16)

    stages = [(half, rh) for half in range(CW // FFT_HALF) for rh in range(2)]
    blocks_per_stage = k1_half // FFT_ROWS
    assert blocks_per_stage == FFT_N2
    for s in range(len(stages) + 1):
        for n2 in range(FFT_N2):
            if s < len(stages):
                mxu_dot(*stages[s], n2)
            if s > 0:
                p_half, p_rh = stages[s - 1]
                row_block(p_half, p_rh * blocks_per_stage + n2)


def _fnet_out_kernel(x_ref, g_ref, w_gate_ref, f_ref, w_out_ref, o_ref, xn_s):
    xn_s[...] = _rms(x_ref[0], g_ref[...]).astype(BF16)
    o_ref[0] = x_ref[0]
    for c in range(NC):
        z = _dot(xn_s[...], w_gate_ref[:, _cols(0, c)])
        v = (f_ref[0, c].astype(F32) * _silu(z)).astype(BF16)
        o_ref[0] += _dot(v, w_out_ref[_cols(0, c), :])


def _resident(shape):
    nd = len(shape)
    return pl.BlockSpec(shape, lambda *_: (0,) * nd, pipeline_mode=pl.Buffered(1))


def _params(sem):
    return pltpu.CompilerParams(dimension_semantics=sem, vmem_limit_bytes=VMEM_LIMIT)


def _row_tile(ts):
    return pl.BlockSpec((1, ts, D_MODEL), lambda b, i: (b, i, 0))


def _tiled_layer(body, name, x, consts, scratch, ts):
    per_tile = ts // HALO
    prev = pl.BlockSpec((1, HALO, D_MODEL), lambda b, i: (b, jnp.maximum(i * per_tile - 1, 0), 0))
    nxt = pl.BlockSpec((1, HALO, D_MODEL),
                       lambda b, i: (b, jnp.minimum((i + 1) * per_tile, SEQ // HALO - 1), 0))
    return pl.pallas_call(
        body,
        grid=(BATCH, SEQ // ts),
        in_specs=[_row_tile(ts), prev, nxt] + [_resident(c.shape) for c in consts],
        out_specs=_row_tile(ts),
        out_shape=jax.ShapeDtypeStruct((BATCH, SEQ, D_MODEL), F32),
        scratch_shapes=[pltpu.VMEM((ts + 2 * HALO, D_MODEL), BF16)] + scratch,
        compiler_params=_params(("arbitrary", "arbitrary")),
        name=name,
    )(x, x, x, *consts)


def _dft_tables(n):
    k = np.arange(n, dtype=np.int64)
    ang = ((k[:, None] * k[None, :]) % n) * (2.0 * np.pi / n)
    return np.cos(ang), np.sin(ang)


def _seq_fft_tables():
    k1 = np.arange(FFT_N1, dtype=np.int64)[:, None]
    n1 = np.arange(FFT_N1, dtype=np.int64)[None, :]
    mats = []
    for n2 in range(FFT_N2):
        ang = ((k1 * (FFT_N2 * n1 + n2)) % SEQ) * (2.0 * np.pi / SEQ)
        c, s = np.cos(ang), np.sin(ang)
        mats.append(np.block([[c, -s], [-s, -c]]))
    return jnp.asarray(np.stack(mats), dtype=BF16)


def _conv_dft_tables():
    n, hw = CONV_N, CONV_N // 2
    t = np.arange(n, dtype=np.int64)
    f = np.arange(hw, dtype=np.int64)
    ang = ((f[:, None] * t[None, :]) % n) * (2.0 * np.pi / n)
    fwd = np.concatenate([np.cos(ang), -np.sin(ang)], axis=0)
    fwd[hw] = np.cos(np.pi * t)
    inv = np.concatenate([np.cos(ang.T), -np.sin(ang.T)], axis=1) * (2.0 / n)
    inv[:, 0] = 1.0 / n
    inv[:, hw] = np.cos(np.pi * t) / n
    lag = CONF_KERNEL // 2 - np.arange(CONF_KERNEL, dtype=np.int64)
    angf = ((f[:, None] * lag[None, :]) % n) * (2.0 * np.pi / n)
    re_bot = np.cos(angf)
    re_bot[0] = np.cos(np.pi * lag)
    trig = np.zeros((n + hw, LANES))
    trig[:, :CONF_KERNEL] = np.concatenate([np.cos(angf), re_bot, -np.sin(angf)], axis=0)
    return jnp.asarray(fwd, dtype=BF16), jnp.asarray(inv, dtype=BF16), jnp.asarray(trig, dtype=F32)


def _fnet_layer(x, g, w_branch, w_gate, w_mix, b_mix, w_out):
    gd = FNET_GROUP_DIM
    ortho = 1.0 / (SEQ * gd) ** 0.5
    cc, sc = _dft_tables(gd)
    ab = pl.pallas_call(
        _fnet_mix_weights_kernel,
        grid=(1,),
        in_specs=[pl.BlockSpec((gd, gd), lambda i: (0, 0)), pl.BlockSpec((gd, gd), lambda i: (0, 0)),
                  pl.BlockSpec((FNET_GROUPS, gd, gd), lambda i: (0, 0, 0))],
        out_specs=pl.BlockSpec((FNET_GROUPS, gd, 2 * gd), lambda i: (0, 0, 0)),
        out_shape=jax.ShapeDtypeStruct((FNET_GROUPS, gd, 2 * gd), BF16),
        compiler_params=_params(("arbitrary",)),
        name="fnet_mix_weights",
    )(jnp.asarray(cc * ortho, dtype=BF16), jnp.asarray(sc * ortho, dtype=BF16), w_mix)

    ts = TS_WIDE
    pq = pl.pallas_call(
        _fnet_in_kernel,
        grid=(BATCH, SEQ // ts),
        in_specs=[_row_tile(ts), _resident(g.shape), _resident(w_branch.shape), _resident(ab.shape)],
        out_specs=pl.BlockSpec((1, NC, FFT_N2, 2, ts // FFT_N2, CW), lambda b, i: (b, 0, 0, 0, i, 0)),
        out_shape=jax.ShapeDtypeStruct((BATCH, NC, FFT_N2, 2, FFT_N1, CW), BF16),
        scratch_shapes=[pltpu.VMEM((ts, D_MODEL), BF16), pltpu.VMEM((2 * 2 * LT, ts, LANES), F32)],
        compiler_params=_params(("arbitrary", "arbitrary")),
        name="fnet_in",
    )(x, g, w_branch, ab)

    m = _seq_fft_tables()
    f = pl.pallas_call(
        _fnet_seq_dft_kernel,
        grid=(BATCH, NC),
        in_specs=[_resident(m.shape),
                  pl.BlockSpec((1, 1, FFT_N2, 2, FFT_N1, CW), lambda b, c: (b, c, 0, 0, 0, 0)),
                  pl.BlockSpec((1, 1, CW), lambda b, c: (c, 0, 0))],
        out_specs=pl.BlockSpec((1, 1, SEQ, CW), lambda b, c: (b, c, 0, 0)),
        out_shape=jax.ShapeDtypeStruct((BATCH, NC, SEQ, CW), BF16),
        scratch_shapes=[pltpu.VMEM((CW // FFT_HALF, FFT_N2, 2 * FFT_N1, FFT_HALF), F32)],
        compiler_params=_params(("arbitrary", "arbitrary")),
        name="fnet_seq_dft",
    )(m, pq, b_mix.reshape(NC, 1, CW))

    return pl.pallas_call(
        _fnet_out_kernel,
        grid=(BATCH, SEQ // ts),
        in_specs=[_row_tile(ts), _resident(g.shape), _resident(w_gate.shape),
                  pl.BlockSpec((1, NC, ts, CW), lambda b, i: (b, 0, i, 0)), _resident(w_out.shape)],
        out_specs=_row_tile(ts),
        out_shape=jax.ShapeDtypeStruct((BATCH, SEQ, D_MODEL), F32),
        scratch_shapes=[pltpu.VMEM((ts, D_MODEL), BF16)],
        compiler_params=_params(("arbitrary", "arbitrary")),
        name="fnet_out",
    )(x, g, w_gate, f, w_out)


def kernel(x, norm_g, w_out, final_g, fnet_w_in, fnet_w_mix, fnet_b_mix, conf_w_in, conf_dw_w, conf_dw_b, conf_ln_g, conf_ln_b, pool_w_in, pool_w_grp, pool_scale, sc_w_in, sc_conv_w):
    assert x.shape == (BATCH, SEQ, D_MODEL) and x.dtype == F32
    assert fnet_w_in.shape[0] == conf_w_in.shape[0] == pool_w_in.shape[0] == sc_w_in.shape[0] == 1
    g = norm_g.reshape(4, 1, D_MODEL)
    w_out_b = w_out.astype(BF16)
    slabs = pltpu.VMEM((NC * LT, TS_WIDE + 2 * HALO, LANES), F32)

    x = _fnet_layer(x, g[0], fnet_w_in[0, :, :D_INNER].astype(BF16), fnet_w_in[0, :, D_INNER:].astype(BF16),
                    fnet_w_mix[0], fnet_b_mix[0], w_out_b[0])

    fwd, inv, trig = _conv_dft_tables()
    dw_w = jnp.pad(conf_dw_w[0], ((0, LANES - CONF_KERNEL), (0, 0)))
    gtab = pl.pallas_call(
        _conf_filter_kernel,
        grid=(1,),
        in_specs=[pl.BlockSpec(trig.shape, lambda i: (0, 0)),
                  pl.BlockSpec((LANES, D_INNER), lambda i: (0, 0))],
        out_specs=pl.BlockSpec((trig.shape[0], D_INNER), lambda i: (0, 0)),
        out_shape=jax.ShapeDtypeStruct((trig.shape[0], D_INNER), F32),
        compiler_params=_params(("arbitrary",)),
        name="conformer_filter_spectrum",
    )(trig, dw_w)
    x = _tiled_layer(
        _conf_kernel, "conformer_layer", x,
        [g[1], conf_w_in[0].astype(BF16), fwd, inv, gtab, conf_dw_b, conf_ln_g, conf_ln_b, w_out_b[1]],
        [pltpu.VMEM((NC, TS, CW), F32)], TS)

    pool_w = pool_w_in[0].astype(BF16)
    w_fold = pl.pallas_call(
        _pool_fold_kernel,
        grid=(1,),
        in_specs=[pl.BlockSpec((D_MODEL, D_INNER), lambda i: (0, 0)),
                  pl.BlockSpec((NC, CW, CW), lambda i: (0, 0, 0))],
        out_specs=pl.BlockSpec((D_MODEL, D_INNER), lambda i: (0, 0)),
        out_shape=jax.ShapeDtypeStruct((D_MODEL, D_INNER), BF16),
        compiler_params=_params(("arbitrary",)),
        name="pool_fold_weights",
    )(pool_w, pool_w_grp[0])
    x = _tiled_layer(
        _pool_kernel, "pool_layer", x,
        [g[2], w_fold, pool_w[:, D_INNER:], pool_scale, w_out_b[2]],
        [slabs], TS_WIDE)

    x = _tiled_layer(
        _sconv_kernel, "short_conv_layer", x,
        [g[3], sc_w_in[0].astype(BF16), sc_conv_w[0], w_out_b[3], final_g.reshape(1, D_MODEL)],
        [slabs], TS_WIDE)
    return x
```

```python
import jax
import jax.numpy as jnp
import numpy as np
from jax import lax
from jax.experimental import pallas as pl
from jax.experimental.pallas import tpu as pltpu

D_MODEL = 1024
D_INNER = 2048
BATCH = 8
SEQ = 2048
FNET_GROUPS = 4
FNET_GROUP_DIM = D_INNER // FNET_GROUPS
CONF_KERNEL = 31
POOL_WINDOWS = (2, 4, 8, 16)
SHORT_CONV_WIDTH = 3
RMS_EPS = 1e-6
LN_EPS = 1e-5

TS = 512
TS_WIDE = 1024
HALO = 16
EXT = TS + 2 * HALO
CONV_N = 256
CONV_BLOCKS = ((0, 16, 240), (208, 32, 240), (288, 160, 240))
CW = 512
NC = D_INNER // CW
LANES = 128
LT = CW // LANES
FFT_N1 = 256
FFT_N2 = SEQ // FFT_N1
FFT_ROWS = 16
FFT_HALF = 256
VMEM_LIMIT = 56 * 1024 * 1024

F32 = jnp.float32
BF16 = jnp.bfloat16


def _dot(a, b):
    return jnp.dot(a, b, preferred_element_type=F32)


def _rms(v, g):
    ms = jnp.mean(v * v, axis=-1, keepdims=True)
    return v * lax.rsqrt(ms + RMS_EPS) * g


def _sigmoid(v):
    return 0.5 * jnp.tanh(0.5 * v) + 0.5


def _silu(v):
    return v * _sigmoid(v)


def _cols(part, c):
    start = part * D_INNER + c * CW
    return slice(start, start + CW)


def _fill_xn_ext(xm_ref, xp_ref, xq_ref, g_ref, xn_s):
    i = pl.program_id(1)
    ts = xm_ref.shape[1]
    g = g_ref[...]
    prev = jnp.where(i > 0, _rms(xp_ref[0], g), 0.0)
    nxt = jnp.where(i < pl.num_programs(1) - 1, _rms(xq_ref[0], g), 0.0)
    xn_s[0:HALO, :] = prev.astype(BF16)
    xn_s[HALO:HALO + ts, :] = _rms(xm_ref[0], g).astype(BF16)
    xn_s[HALO + ts:ts + 2 * HALO, :] = nxt.astype(BF16)


def _store_lane_tiles(slab_ref, first, val):
    for lt in range(val.shape[-1] // LANES):
        slab_ref[first + lt] = val[:, lt * LANES:(lt + 1) * LANES]


def _conf_filter_kernel(trig_ref, w_ref, o_ref):
    o_ref[...] = jnp.dot(trig_ref[...], w_ref[...], preferred_element_type=F32,
                         precision=lax.Precision.HIGHEST)


def _conf_kernel(xm_ref, xp_ref, xq_ref, g_ref, w_in_ref, fwd_ref, inv_ref, gtab_ref, dw_b_ref,
                 ln_g_ref, ln_b_ref, w_out_ref, o_ref, xn_s, hc_s):
    _fill_xn_ext(xm_ref, xp_ref, xq_ref, g_ref, xn_s)
    hw = CONV_N // 2

    for c in range(NC):
        cs = _cols(0, c)
        xe = xn_s[...]
        a = _dot(xe, w_in_ref[:, _cols(0, c)])
        gate = _dot(xe, w_in_ref[:, _cols(1, c)])
        h = (a * _sigmoid(gate)).astype(BF16)
        nb = len(CONV_BLOCKS)
        tile3 = lambda v: jnp.concatenate([v] * nb, axis=-1)
        g_re_top = tile3(gtab_ref[0:hw, cs])
        g_re_bot = tile3(gtab_ref[hw:CONV_N, cs])
        g_im = tile3(gtab_ref[CONV_N:CONV_N + hw, cs])
        bias = dw_b_ref[:, cs]
        hb = jnp.concatenate([h[off:off + CONV_N] for off, _, _ in CONV_BLOCKS], axis=-1)
        spec = _dot(fwd_ref[...], hb)
        top, bot = spec[0:hw], spec[hw:CONV_N]
        y = jnp.concatenate([top * g_re_top - bot * g_im, top * g_im + bot * g_re_bot], axis=0)
        conv = _dot(inv_ref[...], y.astype(BF16))
        for j, (off, lo, hi) in enumerate(CONV_BLOCKS):
            t0 = off + lo - HALO
            hc_s[c, t0:t0 + hi - lo, :] = conv[lo:hi, j * CW:(j + 1) * CW] + bias

    tot = jnp.sum(hc_s[0], axis=-1, keepdims=True)
    for c in range(1, NC):
        tot = tot + jnp.sum(hc_s[c], axis=-1, keepdims=True)
    mu = tot * (1.0 / D_INNER)
    sq = jnp.sum(jnp.square(hc_s[0] - mu), axis=-1, keepdims=True)
    for c in range(1, NC):
        sq = sq + jnp.sum(jnp.square(hc_s[c] - mu), axis=-1, keepdims=True)
    rstd = lax.rsqrt(sq * (1.0 / D_INNER) + LN_EPS)

    o_ref[0] = xm_ref[0]
    for c in range(NC):
        cs = _cols(0, c)
        hn = (hc_s[c] - mu) * rstd * ln_g_ref[:, cs] + ln_b_ref[:, cs]
        z = _dot(xn_s[HALO:HALO + TS, :], w_in_ref[:, _cols(2, c)])
        v = (_silu(hn) * _silu(z)).astype(BF16)
        o_ref[0] += _dot(v, w_out_ref[cs, :])


def _pool_fold_kernel(w_in_ref, w_grp_ref, o_ref):
    o_ref[...] = _dot(w_in_ref[...], w_grp_ref[0].astype(BF16)).astype(BF16)


def _pool_kernel(xm_ref, xp_ref, xq_ref, g_ref, w_fold_ref, w_gate_ref, scale_ref, w_out_ref, o_ref,
                 xn_s, u_s):
    _fill_xn_ext(xm_ref, xp_ref, xq_ref, g_ref, xn_s)
    ts = xm_ref.shape[1]
    t = pl.program_id(1) * ts + lax.broadcasted_iota(jnp.int32, (ts, 1), 0)
    o_ref[0] = xm_ref[0]

    for c, w in enumerate(POOL_WINDOWS):
        cs = _cols(0, c)
        left = w // 2
        right = w - 1 - left
        _store_lane_tiles(u_s, c * LT, _dot(xn_s[...], w_fold_ref[:, cs]))
        cnt = jnp.minimum(t + right + 1, SEQ) - jnp.maximum(t - left, 0)
        inv_cnt = 1.0 / cnt.astype(F32)
        parts = []
        for lt in range(LT):
            slab = c * LT + lt
            wsum = u_s[slab, HALO - left:HALO - left + ts, :]
            for k in range(1, w):
                wsum = wsum + u_s[slab, HALO - left + k:HALO - left + k + ts, :]
            parts.append(wsum * inv_cnt - u_s[slab, HALO:HALO + ts, :])
        y = jnp.concatenate(parts, axis=-1) * scale_ref[:, cs]
        z = _dot(xn_s[HALO:HALO + ts, :], w_gate_ref[:, cs])
        v = (y * _silu(z)).astype(BF16)
        o_ref[0] += _dot(v, w_out_ref[cs, :])


def _sconv_kernel(xm_ref, xp_ref, xq_ref, g_ref, w_in_ref, conv_w_ref, w_out_ref, fg_ref, o_ref,
                  xn_s, ch_s):
    _fill_xn_ext(xm_ref, xp_ref, xq_ref, g_ref, xn_s)
    ts = xm_ref.shape[1]
    o_ref[0] = xm_ref[0]
    base = HALO - SHORT_CONV_WIDTH // 2

    for c in range(NC):
        cs = _cols(0, c)
        xe = xn_s[...]
        cg = _dot(xe, w_in_ref[:, _cols(1, c)])
        hh = _dot(xe, w_in_ref[:, _cols(2, c)])
        _store_lane_tiles(ch_s, c * LT, cg * hh)
        parts = []
        for lt in range(LT):
            slab = c * LT + lt
            ls = slice(c * CW + lt * LANES, c * CW + (lt + 1) * LANES)
            conv = ch_s[slab, base:base + ts, :] * conv_w_ref[0:1, ls]
            for k in range(1, SHORT_CONV_WIDTH):
                conv = conv + ch_s[slab, base + k:base + k + ts, :] * conv_w_ref[k:k + 1, ls]
            parts.append(conv)
        conv = jnp.concatenate(parts, axis=-1)
        xm = xn_s[HALO:HALO + ts, :]
        bg = _dot(xm, w_in_ref[:, cs])
        z = _dot(xm, w_in_ref[:, _cols(3, c)])
        v = (bg * conv * _silu(z)).astype(BF16)
        o_ref[0] += _dot(v, w_out_ref[cs, :])
    o_ref[0] = _rms(o_ref[0], fg_ref[...])


def _fnet_mix_weights_kernel(cc_ref, sc_ref, w_ref, o_ref):
    w = w_ref[0].astype(BF16)
    gd = FNET_GROUP_DIM
    o_ref[0, :, 0:gd] = _dot(cc_ref[...], w).astype(BF16)
    o_ref[0, :, gd:2 * gd] = _dot(sc_ref[...], w).astype(BF16)


def _fnet_in_kernel(x_ref, g_ref, w_in_ref, ab_ref, pq_ref, xn_s, pq_s):
    xn_s[...] = _rms(x_ref[0], g_ref[...]).astype(BF16)
    n1_rows = x_ref.shape[1] // FFT_N2
    slab_sets = pq_s.shape[0] // (2 * LT)

    for c in range(NC):
        first = (c % slab_sets) * 2 * LT
        u = _dot(xn_s[...], w_in_ref[:, _cols(0, c)]).astype(BF16)
        _store_lane_tiles(pq_s, first, _dot(u, ab_ref[c]))
        for n2 in range(FFT_N2):
            for part in range(2):
                for lt in range(LT):
                    piece = pq_s[first + part * LT + lt, pl.ds(n2, n1_rows, stride=FFT_N2), :]
                    pq_ref[0, c, n2, part, :, lt * LANES:(lt + 1) * LANES] = piece.astype(BF16)


def _fnet_seq_dft_kernel(m_ref, pq_ref, b_ref, f_ref, a_s):
    rt = 0.5 ** 0.5

    k1_half = FFT_N1 // 2

    def mxu_dot(half, rh, n2):
        hl = slice(half * FFT_HALF, (half + 1) * FFT_HALF)
        re = slice(rh * k1_half, (rh + 1) * k1_half)
        im = slice(FFT_N1 + rh * k1_half, FFT_N1 + (rh + 1) * k1_half)
        rhs = pq_ref[0, 0, n2, :, :, hl].reshape(2 * FFT_N1, FFT_HALF)
        if n2 % (FFT_N2 // 2) == 0:
            a_s[half, n2, re, :] = _dot(m_ref[n2, re, :], rhs)
            return
        res = _dot(jnp.concatenate([m_ref[n2, re, :], m_ref[n2, im, :]], axis=0), rhs)
        a_s[half, n2, re, :] = res[0:k1_half]
        a_s[half, n2, im, :] = res[k1_half:2 * k1_half]

    def row_block(half, rb):
        r0 = rb * FFT_ROWS
        re_rows = pl.ds(r0, FFT_ROWS)
        im_rows = pl.ds(FFT_N1 + r0, FFT_ROWS)
        for lt in range(FFT_HALF // LANES):
            ls = slice(lt * LANES, (lt + 1) * LANES)
            a = [a_s[half, n2, re_rows, ls] for n2 in range(FFT_N2)]
            b = [None if n2 % (FFT_N2 // 2) == 0 else a_s[half, n2, im_rows, ls] for n2 in range(FFT_N2)]
            s04, d04 = a[0] + a[4], a[0] - a[4]
            p1, p2, p3 = a[1] + a[7], a[2] + a[6], a[3] + a[5]
            p13, t = p1 + p3, rt * (p1 - p3)
            e = s04 + p2
            q1, q2, q3 = b[1] - b[7], b[2] - b[6], b[3] - b[5]
            u = rt * (q1 + q3)
            ca = [e + p13, d04 + t, s04 - p2, d04 - t, e - p13]
            sb = [None, q2 + u, q1 - q3, u - q2]
            out = [ca[0], ca[1] + sb[1], ca[2] + sb[2], ca[3] + sb[3], ca[4],
                   ca[3] - sb[3], ca[2] - sb[2], ca[1] - sb[1]]
            os = slice(half * FFT_HALF + lt * LANES, half * FFT_HALF + (lt + 1) * LANES)
            bias = b_ref[0, :, os]
            for k2 in range(FFT_N2):
                rows = pl.ds(k2 * FFT_N1 + r0, FFT_ROWS)
                f_ref[0, 0, rows, os] = (out[k2] + bias).astype(BF16)

    stages = [(half, rh) for half in range(CW // FFT_HALF) for rh in range(2)]
    blocks_per_stage = k1_half // FFT_ROWS
    assert blocks_per_stage == FFT_N2
    for s in range(len(stages) + 1):
        for n2 in range(FFT_N2):
            if s < len(stages):
                mxu_dot(*stages[s], n2)
            if s > 0:
                p_half, p_rh = stages[s - 1]
                row_block(p_half, p_rh * blocks_per_stage + n2)


def _fnet_out_kernel(x_ref, g_ref, w_gate_ref, f_ref, w_out_ref, o_ref, xn_s):
    xn_s[...] = _rms(x_ref[0], g_ref[...]).astype(BF16)
    o_ref[0] = x_ref[0]
    for c in range(NC):
        z = _dot(xn_s[...], w_gate_ref[:, _cols(0, c)])
        v = (f_ref[0, c].astype(F32) * _silu(z)).astype(BF16)
        o_ref[0] += _dot(v, w_out_ref[_cols(0, c), :])


def _resident(shape):
    nd = len(shape)
    return pl.BlockSpec(shape, lambda *_: (0,) * nd, pipeline_mode=pl.Buffered(1))


def _params(sem):
    return pltpu.CompilerParams(dimension_semantics=sem, vmem_limit_bytes=VMEM_LIMIT)


def _row_tile(ts):
    return pl.BlockSpec((1, ts, D_MODEL), lambda b, i: (b, i, 0))


def _tiled_layer(body, name, x, consts, scratch, ts):
    per_tile = ts // HALO
    prev = pl.BlockSpec((1, HALO, D_MODEL), lambda b, i: (b, jnp.maximum(i * per_tile - 1, 0), 0))
    nxt = pl.BlockSpec((1, HALO, D_MODEL),
                       lambda b, i: (b, jnp.minimum((i + 1) * per_tile, SEQ // HALO - 1), 0))
    return pl.pallas_call(
        body,
        grid=(BATCH, SEQ // ts),
        in_specs=[_row_tile(ts), prev, nxt] + [_resident(c.shape) for c in consts],
        out_specs=_row_tile(ts),
        out_shape=jax.ShapeDtypeStruct((BATCH, SEQ, D_MODEL), F32),
        scratch_shapes=[pltpu.VMEM((ts + 2 * HALO, D_MODEL), BF16)] + scratch,
        compiler_params=_params(("arbitrary", "arbitrary")),
        name=name,
    )(x, x, x, *consts)


def _dft_tables(n):
    k = np.arange(n, dtype=np.int64)
    ang = ((k[:, None] * k[None, :]) % n) * (2.0 * np.pi / n)
    return np.cos(ang), np.sin(ang)


def _seq_fft_tables():
    k1 = np.arange(FFT_N1, dtype=np.int64)[:, None]
    n1 = np.arange(FFT_N1, dtype=np.int64)[None, :]
    mats = []
    for n2 in range(FFT_N2):
        ang = ((k1 * (FFT_N2 * n1 + n2)) % SEQ) * (2.0 * np.pi / SEQ)
        c, s = np.cos(ang), np.sin(ang)
        mats.append(np.block([[c, -s], [-s, -c]]))
    return jnp.asarray(np.stack(mats), dtype=BF16)


def _conv_dft_tables():
    n, hw = CONV_N, CONV_N // 2
    t = np.arange(n, dtype=np.int64)
    f = np.arange(hw, dtype=np.int64)
    ang = ((f[:, None] * t[None, :]) % n) * (2.0 * np.pi / n)
    fwd = np.concatenate([np.cos(ang), -np.sin(ang)], axis=0)
    fwd[hw] = np.cos(np.pi * t)
    inv = np.concatenate([np.cos(ang.T), -np.sin(ang.T)], axis=1) * (2.0 / n)
    inv[:, 0] = 1.0 / n
    inv[:, hw] = np.cos(np.pi * t) / n
    lag = CONF_KERNEL // 2 - np.arange(CONF_KERNEL, dtype=np.int64)
    angf = ((f[:, None] * lag[None, :]) % n) * (2.0 * np.pi / n)
    re_bot = np.cos(angf)
    re_bot[0] = np.cos(np.pi * lag)
    trig = np.zeros((n + hw, LANES))
    trig[:, :CONF_KERNEL] = np.concatenate([np.cos(angf), re_bot, -np.sin(angf)], axis=0)
    return jnp.asarray(fwd, dtype=BF16), jnp.asarray(inv, dtype=BF16), jnp.asarray(trig, dtype=F32)


def _fnet_layer(x, g, w_branch, w_gate, w_mix, b_mix, w_out):
    gd = FNET_GROUP_DIM
    ortho = 1.0 / (SEQ * gd) ** 0.5
    cc, sc = _dft_tables(gd)
    ab = pl.pallas_call(
        _fnet_mix_weights_kernel,
        grid=(FNET_GROUPS,),
        in_specs=[_resident((gd, gd)), _resident((gd, gd)),
                  pl.BlockSpec((1, gd, gd), lambda c: (c, 0, 0))],
        out_specs=pl.BlockSpec((1, gd, 2 * gd), lambda c: (c, 0, 0)),
        out_shape=jax.ShapeDtypeStruct((FNET_GROUPS, gd, 2 * gd), BF16),
        compiler_params=_params(("arbitrary",)),
        name="fnet_mix_weights",
    )(jnp.asarray(cc * ortho, dtype=BF16), jnp.asarray(sc * ortho, dtype=BF16), w_mix)

    ts = TS_WIDE
    pq = pl.pallas_call(
        _fnet_in_kernel,
        grid=(BATCH, SEQ // ts),
        in_specs=[_row_tile(ts), _resident(g.shape), _resident(w_branch.shape), _resident(ab.shape)],
        out_specs=pl.BlockSpec((1, NC, FFT_N2, 2, ts // FFT_N2, CW), lambda b, i: (b, 0, 0, 0, i, 0)),
        out_shape=jax.ShapeDtypeStruct((BATCH, NC, FFT_N2, 2, FFT_N1, CW), BF16),
        scratch_shapes=[pltpu.VMEM((ts, D_MODEL), BF16), pltpu.VMEM((2 * 2 * LT, ts, LANES), F32)],
        compiler_params=_params(("arbitrary", "arbitrary")),
        name="fnet_in",
    )(x, g, w_branch, ab)

    m = _seq_fft_tables()
    f = pl.pallas_call(
        _fnet_seq_dft_kernel,
        grid=(BATCH, NC),
        in_specs=[_resident(m.shape),
                  pl.BlockSpec((1, 1, FFT_N2, 2, FFT_N1, CW), lambda b, c: (b, c, 0, 0, 0, 0)),
                  pl.BlockSpec((1, 1, CW), lambda b, c: (c, 0, 0))],
        out_specs=pl.BlockSpec((1, 1, SEQ, CW), lambda b, c: (b, c, 0, 0)),
        out_shape=jax.ShapeDtypeStruct((BATCH, NC, SEQ, CW), BF16),
        scratch_shapes=[pltpu.VMEM((CW // FFT_HALF, FFT_N2, 2 * FFT_N1, FFT_HALF), F32)],
        compiler_params=_params(("arbitrary", "arbitrary")),
        name="fnet_seq_dft",
    )(m, pq, b_mix.reshape(NC, 1, CW))

    return pl.pallas_call(
        _fnet_out_kernel,
        grid=(BATCH, SEQ // ts),
        in_specs=[_row_tile(ts), _resident(g.shape), _resident(w_gate.shape),
                  pl.BlockSpec((1, NC, ts, CW), lambda b, i: (b, 0, i, 0)), _resident(w_out.shape)],
        out_specs=_row_tile(ts),
        out_shape=jax.ShapeDtypeStruct((BATCH, SEQ, D_MODEL), F32),
        scratch_shapes=[pltpu.VMEM((ts, D_MODEL), BF16)],
        compiler_params=_params(("arbitrary", "arbitrary")),
        name="fnet_out",
    )(x, g, w_gate, f, w_out)


def kernel(x, norm_g, w_out, final_g, fnet_w_in, fnet_w_mix, fnet_b_mix, conf_w_in, conf_dw_w, conf_dw_b, conf_ln_g, conf_ln_b, pool_w_in, pool_w_grp, pool_scale, sc_w_in, sc_conv_w):
    assert x.shape == (BATCH, SEQ, D_MODEL) and x.dtype == F32
    assert fnet_w_in.shape[0] == conf_w_in.shape[0] == pool_w_in.shape[0] == sc_w_in.shape[0] == 1
    g = norm_g.reshape(4, 1, D_MODEL)
    w_out_b = w_out.astype(BF16)
    slabs = pltpu.VMEM((NC * LT, TS_WIDE + 2 * HALO, LANES), F32)

    x = _fnet_layer(x, g[0], fnet_w_in[0, :, :D_INNER].astype(BF16), fnet_w_in[0, :, D_INNER:].astype(BF16),
                    fnet_w_mix[0], fnet_b_mix[0], w_out_b[0])

    fwd, inv, trig = _conv_dft_tables()
    dw_w = jnp.pad(conf_dw_w[0], ((0, LANES - CONF_KERNEL), (0, 0)))
    gtab = pl.pallas_call(
        _conf_filter_kernel,
        grid=(NC,),
        in_specs=[_resident(trig.shape), pl.BlockSpec((LANES, CW), lambda c: (0, c))],
        out_specs=pl.BlockSpec((trig.shape[0], CW), lambda c: (0, c)),
        out_shape=jax.ShapeDtypeStruct((trig.shape[0], D_INNER), F32),
        compiler_params=_params(("arbitrary",)),
        name="conformer_filter_spectrum",
    )(trig, dw_w)
    x = _tiled_layer(
        _conf_kernel, "conformer_layer", x,
        [g[1], conf_w_in[0].astype(BF16), fwd, inv, gtab, conf_dw_b, conf_ln_g, conf_ln_b, w_out_b[1]],
        [pltpu.VMEM((NC, TS, CW), F32)], TS)

    pool_w = pool_w_in[0].astype(BF16)
    w_fold = pl.pallas_call(
        _pool_fold_kernel,
        grid=(NC,),
        in_specs=[pl.BlockSpec((D_MODEL, CW), lambda c: (0, c)),
                  pl.BlockSpec((1, CW, CW), lambda c: (c, 0, 0))],
        out_specs=pl.BlockSpec((D_MODEL, CW), lambda c: (0, c)),
        out_shape=jax.ShapeDtypeStruct((D_MODEL, D_INNER), BF16),
        compiler_params=_params(("arbitrary",)),
        name="pool_fold_weights",
    )(pool_w, pool_w_grp[0])
    x = _tiled_layer(
        _pool_kernel, "pool_layer", x,
        [g[2], w_fold, pool_w[:, D_INNER:], pool_scale, w_out_b[2]],
        [slabs], TS_WIDE)

    x = _tiled_layer(
        _sconv_kernel, "short_conv_layer", x,
        [g[3], sc_w_in[0].astype(BF16), sc_conv_w[0], w_out_b[3], final_g.reshape(1, D_MODEL)],
        [slabs], TS_WIDE)
    return x
```
